```python
import math
import jax, jax.numpy as jnp
from jax import lax
import numpy as np

D_MODEL = 1024
BATCH = 2
SEQ = 8192
DEPTH = 1

N_META = 16
SSM_WIDTH = D_MODEL // 2
SSM_GROUP = 16
SSM_GROUPS = SSM_WIDTH // SSM_GROUP
SSM_STATE = 64
DT_MIN = 1e-3
DT_MAX = 1e-1
A_RE_MAX = -1e-4
ATTN_HEADS = 4
HEAD_DIM = 64
ATTN_WIDTH = ATTN_HEADS * 2 * HEAD_DIM
Q_BLOCK = 128
ATTN_PAD = Q_BLOCK - N_META
NEG_INF = -1e30
N_BRANCH = 2
IN_WIDTH = SSM_WIDTH + 3 * ATTN_WIDTH + N_BRANCH * D_MODEL
N_EXPERTS = 32
TOP_K = 4
D_EXPERT = D_MODEL
SWIGLU_LIMIT = 7.0
SWIGLU_ALPHA = 1.702
ROW_BLOCK = 128
EPS = 1e-6

kernel_name = 'hybrid_s5_diffattn_gated_moe'


def rmsnorm(x, g):
    xf = x.astype(jnp.float32)
    y = xf * lax.rsqrt(jnp.mean(xf * xf, axis=-1, keepdims=True) + EPS)
    return (y * g.astype(jnp.float32)).astype(x.dtype)


def lambda_init_fn(layer):
    return 0.8 - 0.6 * math.exp(-0.3 * layer)


def s5_mixer(u, a_re, a_im, log_dt, b_re, b_im, c_re, c_im, d):
    B, L, _ = u.shape
    f32 = jnp.float32
    uf = u.astype(f32).reshape(B, L, SSM_GROUPS, SSM_GROUP)
    lr = jnp.minimum(a_re.astype(f32), A_RE_MAX)
    li = a_im.astype(f32)
    dt = jnp.exp(log_dt.astype(f32))[:, None]
    mag = jnp.exp(lr * dt)
    ar = mag * jnp.cos(li * dt)
    ai = mag * jnp.sin(li * dt)
    den = lr * lr + li * li
    fr = ((ar - 1.0) * lr + ai * li) / den
    fi = (ai * lr - (ar - 1.0) * li) / den
    fr = fr[..., None]
    fi = fi[..., None]
    br = b_re.astype(f32)
    bi = b_im.astype(f32)
    bbr = fr * br - fi * bi
    bbi = fr * bi + fi * br
    bur = jnp.einsum('blgc,gpc->blgp', uf, bbr)
    bui = jnp.einsum('blgc,gpc->blgp', uf, bbi)
    arb = jnp.broadcast_to(ar, bur.shape)
    aib = jnp.broadcast_to(ai, bur.shape)

    def combine(e1, e2):
        a1r, a1i, b1r, b1i = e1
        a2r, a2i, b2r, b2i = e2
        return (a2r * a1r - a2i * a1i,
                a2r * a1i + a2i * a1r,
                a2r * b1r - a2i * b1i + b2r,
                a2r * b1i + a2i * b1r + b2i)

    _, _, sr, si = lax.associative_scan(combine, (arb, aib, bur, bui), axis=1)
    y = (jnp.einsum('blgp,gcp->blgc', sr, c_re.astype(f32))
         - jnp.einsum('blgp,gcp->blgc', si, c_im.astype(f32)))
    y = y.reshape(B, L, SSM_WIDTH) + d.astype(f32) * uf.reshape(B, L, SSM_WIDTH)
    return y.astype(u.dtype)


def diff_attention(q, k, v, lam, q_g, k_g, sub_g, lambda_init):
    B, L = q.shape[0], q.shape[1]
    H = ATTN_HEADS
    q = rmsnorm(q, q_g)
    k = rmsnorm(k, k_g)
    padf = lambda t: jnp.pad(t, [(0, 0), (ATTN_PAD, 0)] + [(0, 0)] * (t.ndim - 2))
    q, k, v = padf(q), padf(k), padf(v)
    Lp = L + ATTN_PAD
    nb = Lp // Q_BLOCK
    qb = q.reshape(B, nb, Q_BLOCK, H, 2, HEAD_DIM).transpose(1, 0, 3, 4, 2, 5)
    kt = k.transpose(0, 2, 3, 1, 4)
    vt = v.transpose(0, 2, 1, 3)
    kpos = jnp.arange(Lp)
    scale = HEAD_DIM ** -0.5

    def block(args):
        qblk, bi = args
        qpos = bi * Q_BLOCK + jnp.arange(Q_BLOCK)
        mask = (kpos[None, :] <= qpos[:, None]) & (kpos[None, :] >= ATTN_PAD)
        s = jnp.einsum('bhmqd,bhmkd->bhmqk', qblk, kt).astype(jnp.float32) * scale
        s = jnp.where(mask, s, NEG_INF)
        p = jax.nn.softmax(s, axis=-1)
        a = (p[:, :, 0] - lam * p[:, :, 1]).astype(vt.dtype)
        return jnp.einsum('bhqk,bhkv->bhqv', a, vt)

    o = lax.map(block, (qb, jnp.arange(nb)))
    o = o.transpose(1, 0, 3, 2, 4).reshape(B, Lp, H, 2 * HEAD_DIM)[:, ATTN_PAD:]
    o = rmsnorm(o, sub_g) * (1.0 - lambda_init)
    return o.reshape(B, L, ATTN_WIDTH)


def moe_ffn(xn, w_router, b_router, w_gate, b_gate, w_up, b_up, w_down, b_down):
    B, L, D = xn.shape
    T = B * L
    xt = xn.reshape(T, D)
    logits = (xt @ w_router + b_router).astype(jnp.float32)
    top_val, top_idx = lax.top_k(logits, TOP_K)
    gates = jax.nn.softmax(top_val, axis=-1).astype(xn.dtype)
    A = T * TOP_K
    flat_e = top_idx.reshape(A)
    flat_tok = jnp.arange(A, dtype=jnp.int32) // TOP_K
    flat_g = gates.reshape(A)
    order = jnp.argsort(flat_e)
    se = flat_e[order]
    counts = jnp.bincount(flat_e, length=N_EXPERTS)
    padded = (counts + ROW_BLOCK - 1) // ROW_BLOCK * ROW_BLOCK
    start = jnp.cumsum(counts) - counts
    pend = jnp.cumsum(padded)
    pstart = pend - padded
    dest = pstart[se] + jnp.arange(A) - start[se]
    n_blocks = -(-A // ROW_BLOCK) + N_EXPERTS
    n_rows = n_blocks * ROW_BLOCK
    row_tok = jnp.full((n_rows,), T, jnp.int32).at[dest].set(flat_tok[order])
    row_gate = jnp.zeros((n_rows,), xn.dtype).at[dest].set(flat_g[order])
    blk_e = jnp.minimum(jnp.searchsorted(pend, jnp.arange(n_blocks) * ROW_BLOCK, side='right'),
                        N_EXPERTS - 1)
    x_src = jnp.concatenate([xt, jnp.zeros((1, D), xt.dtype)], axis=0)

    def block(args):
        tok, e = args
        hb = x_src[tok]
        g = jnp.minimum(hb @ w_gate[e] + b_gate[e], SWIGLU_LIMIT)
        u = jnp.clip(hb @ w_up[e] + b_up[e], -SWIGLU_LIMIT, SWIGLU_LIMIT)
        a = (u + 1.0) * (g * jax.nn.sigmoid(SWIGLU_ALPHA * g))
        return a @ w_down[e] + b_down[e]

    y = lax.map(block, (row_tok.reshape(n_blocks, ROW_BLOCK), blk_e))
    y = y.reshape(n_rows, D) * row_gate[:, None]
    out = jnp.zeros((T + 1, D), y.dtype).at[row_tok].add(y)[:T]
    return out.reshape(B, L, D)


def setup_inputs(seed: int = 0) -> dict:
    key = jax.random.key(seed)
    ks = jax.random.split(key, 40)
    f = jnp.float32
    nrm = lambda k, shape, s: jax.random.normal(k, shape, f) * s
    G, P, C = SSM_GROUPS, SSM_STATE, SSM_GROUP
    E, F, D = N_EXPERTS, D_EXPERT, D_MODEL
    return {
        'x': nrm(ks[0], (BATCH, SEQ, D), 1.0),
        'meta_tokens': nrm(ks[1], (N_META, D), 1.0),
        'norm_mix_g': 1.0 + nrm(ks[2], (DEPTH, D), 0.02),
        'w_in': nrm(ks[3], (DEPTH, D, IN_WIDTH), D ** -0.5),
        'ssm_a_re': -0.5 + nrm(ks[4], (DEPTH, G, P), 0.01),
        'ssm_a_im': math.pi * jnp.arange(P, dtype=f) + nrm(ks[5], (DEPTH, G, P), 0.01),
        'ssm_log_dt': jax.random.uniform(ks[6], (DEPTH, G), f, math.log(DT_MIN), math.log(DT_MAX)),
        'ssm_b_re': nrm(ks[7], (DEPTH, G, P, C), (2 * C) ** -0.5),
        'ssm_b_im': nrm(ks[8], (DEPTH, G, P, C), (2 * C) ** -0.5),
        'ssm_c_re': nrm(ks[9], (DEPTH, G, C, P), (2 * P) ** -0.5),
        'ssm_c_im': nrm(ks[10], (DEPTH, G, C, P), (2 * P) ** -0.5),
        'ssm_d': nrm(ks[11], (DEPTH, SSM_WIDTH), 1.0),
        'w_glu': nrm(ks[12], (DEPTH, SSM_WIDTH, SSM_WIDTH), SSM_WIDTH ** -0.5),
        'b_glu': nrm(ks[13], (DEPTH, SSM_WIDTH), 0.01),
        'q_norm_g': 1.0 + nrm(ks[14], (DEPTH, HEAD_DIM), 0.02),
        'k_norm_g': 1.0 + nrm(ks[15], (DEPTH, HEAD_DIM), 0.02),
        'lambda_q1': nrm(ks[16], (DEPTH, HEAD_DIM), 0.1),
        'lambda_k1': nrm(ks[17], (DEPTH, HEAD_DIM), 0.1),
        'lambda_q2': nrm(ks[18], (DEPTH, HEAD_DIM), 0.1),
        'lambda_k2': nrm(ks[19], (DEPTH, HEAD_DIM), 0.1),
        'subln_g': 1.0 + nrm(ks[20], (DEPTH, 2 * HEAD_DIM), 0.02),
        'w_ssm_out': nrm(ks[21], (DEPTH, SSM_WIDTH, D), SSM_WIDTH ** -0.5),
        'w_attn_out': nrm(ks[22], (DEPTH, ATTN_WIDTH, D), ATTN_WIDTH ** -0.5),
        'w_o': nrm(ks[23], (DEPTH, D, D), D ** -0.5),
        'norm_ffn_g': 1.0 + nrm(ks[24], (DEPTH, D), 0.02),
        'w_router': nrm(ks[25], (DEPTH, D, E), D ** -0.5),
        'b_router': nrm(ks[26], (DEPTH, E), 0.01),
        'w_gate': nrm(ks[27], (DEPTH, E, D, F), D ** -0.5),
        'b_gate': nrm(ks[28], (DEPTH, E, F), 0.01),
        'w_up': nrm(ks[29], (DEPTH, E, D, F), D ** -0.5),
        'b_up': nrm(ks[30], (DEPTH, E, F), 0.01),
        'w_down': nrm(ks[31], (DEPTH, E, F, D), F ** -0.5),
        'b_down': nrm(ks[32], (DEPTH, E, D), 0.01),
    }


def reference(x, meta_tokens, norm_mix_g, w_in, ssm_a_re, ssm_a_im, ssm_log_dt, ssm_b_re, ssm_b_im,
              ssm_c_re, ssm_c_im, ssm_d, w_glu, b_glu, q_norm_g, k_norm_g, lambda_q1, lambda_k1,
              lambda_q2, lambda_k2, subln_g, w_ssm_out, w_attn_out, w_o, norm_ffn_g, w_router,
              b_router, w_gate, b_gate, w_up, b_up, w_down, b_down):
    B = x.shape[0]
    meta = jnp.broadcast_to(meta_tokens[None].astype(x.dtype), (B, N_META, D_MODEL))
    h = jnp.concatenate([meta, x], axis=1)
    L = h.shape[1]
    o_q = SSM_WIDTH
    o_k = o_q + ATTN_WIDTH
    o_v = o_k + ATTN_WIDTH
    o_g = o_v + ATTN_WIDTH
    for l in range(DEPTH):
        lambda_init = lambda_init_fn(l)
        n = rmsnorm(h, norm_mix_g[l])
        z = n @ w_in[l]
        u = z[..., :o_q]
        q = z[..., o_q:o_k].reshape(B, L, ATTN_HEADS, 2, HEAD_DIM)
        k = z[..., o_k:o_v].reshape(B, L, ATTN_HEADS, 2, HEAD_DIM)
        v = z[..., o_v:o_g].reshape(B, L, ATTN_HEADS, 2 * HEAD_DIM)
        gate_logits = z[..., o_g:].reshape(B, L, N_BRANCH, D_MODEL)
        y_s = s5_mixer(u, ssm_a_re[l], ssm_a_im[l], ssm_log_dt[l], ssm_b_re[l], ssm_b_im[l],
                       ssm_c_re[l], ssm_c_im[l], ssm_d[l])
        y_s = jax.nn.gelu(y_s)
        y_s = y_s * jax.nn.sigmoid(y_s @ w_glu[l] + b_glu[l])
        y_s = y_s @ w_ssm_out[l]
        lam = (jnp.exp(jnp.sum(lambda_q1[l].astype(jnp.float32) * lambda_k1[l].astype(jnp.float32)))
               - jnp.exp(jnp.sum(lambda_q2[l].astype(jnp.float32) * lambda_k2[l].astype(jnp.float32)))
               + lambda_init)
        y_a = diff_attention(q, k, v, lam, q_norm_g[l], k_norm_g[l], subln_g[l], lambda_init)
        y_a = y_a @ w_attn_out[l]
        g = jax.nn.sigmoid(gate_logits)
        mixed = g[:, :, 0] * y_s + g[:, :, 1] * y_a
        h = h + mixed @ w_o[l]
        h = h + moe_ffn(rmsnorm(h, norm_ffn_g[l]), w_router[l], b_router[l], w_gate[l], b_gate[l],
                        w_up[l], b_up[l], w_down[l], b_down[l])
    return h[:, N_META:]
```

```python
import functools
import math

import numpy as np
import jax
import jax.numpy as jnp
from jax import lax
from jax.experimental import pallas as pl
from jax.experimental.pallas import tpu as pltpu

F32 = jnp.float32
BF16 = jnp.bfloat16

D_MODEL = 1024
N_META = 16
SSM_WIDTH = 512
SSM_GROUP = 16
SSM_GROUPS = 32
SSM_STATE = 64
A_RE_MAX = -1e-4
ATTN_HEADS = 4
HEAD_DIM = 64
ATTN_WIDTH = 512
NEG_INF = -1e30
N_EXPERTS = 32
TOP_K = 4
SWIGLU_LIMIT = 7.0
SWIGLU_ALPHA = 1.702
EPS = 1e-6
LAMBDA_INIT = 0.8 - 0.6 * math.exp(-0.3 * 0)

LANES = 128
SUBLANES = 8
META_ROWS = 128
VMEM_LIMIT = 56 * 1024 * 1024

N_STATE = SSM_GROUPS * SSM_STATE
STATE_TILES = N_STATE // LANES


def _cparams(*sem):
    return pltpu.CompilerParams(dimension_semantics=sem, vmem_limit_bytes=VMEM_LIMIT)


def _in_proj_kernel(x_ref, g_ref, w_ref, qg_ref, kg_ref, gm_ref, u_ref, q_ref, k_ref, v_ref):
    x = x_ref[...]
    ms = jnp.mean(x * x, axis=-1, keepdims=True)
    n = (x * lax.rsqrt(ms + EPS) * g_ref[...]).astype(BF16)
    z = jnp.dot(n, w_ref[...], preferred_element_type=F32)
    u_ref[...] = z[:, :SSM_WIDTH]
    gm = gm_ref[...]

    def head_norm(t, g):
        ms_g = jnp.dot((t * t).astype(BF16), gm, preferred_element_type=F32)
        return t * lax.rsqrt(ms_g + EPS) * g

    q = z[:, SSM_WIDTH:SSM_WIDTH + ATTN_WIDTH]
    k = z[:, SSM_WIDTH + ATTN_WIDTH:SSM_WIDTH + 2 * ATTN_WIDTH]
    q_ref[...] = head_norm(q, qg_ref[...]).astype(BF16)
    k_ref[...] = head_norm(k, kg_ref[...]).astype(BF16)
    v_ref[...] = z[:, SSM_WIDTH + 2 * ATTN_WIDTH:].astype(BF16)


def _in_proj(x2, g_mix, w_uqkv, qg, kg, gmat, tm):
    n_rows = x2.shape[0]
    wdt = w_uqkv.shape[1]
    row = lambda i: (i, 0)
    fixed = lambda i: (0, 0)
    return pl.pallas_call(
        _in_proj_kernel,
        grid=(n_rows // tm,),
        in_specs=[
            pl.BlockSpec((tm, D_MODEL), row),
            pl.BlockSpec((1, D_MODEL), fixed),
            pl.BlockSpec((D_MODEL, wdt), fixed),
            pl.BlockSpec((1, ATTN_WIDTH), fixed),
            pl.BlockSpec((1, ATTN_WIDTH), fixed),
            pl.BlockSpec((ATTN_WIDTH, ATTN_WIDTH), fixed),
        ],
        out_specs=[
            pl.BlockSpec((tm, SSM_WIDTH), row),
            pl.BlockSpec((tm, ATTN_WIDTH), row),
            pl.BlockSpec((tm, ATTN_WIDTH), row),
            pl.BlockSpec((tm, ATTN_WIDTH), row),
        ],
        out_shape=[
            jax.ShapeDtypeStruct((n_rows, SSM_WIDTH), F32),
            jax.ShapeDtypeStruct((n_rows, ATTN_WIDTH), BF16),
            jax.ShapeDtypeStruct((n_rows, ATTN_WIDTH), BF16),
            jax.ShapeDtypeStruct((n_rows, ATTN_WIDTH), BF16),
        ],
        compiler_params=_cparams("arbitrary"),
        name="in_proj",
    )(x2, g_mix, w_uqkv, qg, kg, gmat)


def _s5_params_kernel(are_ref, aim_ref, ldt_ref, bre_ref, bim_ref,
                      pwr_ref, pwi_ref, bbr_ref, bbi_ref):
    lr = jnp.minimum(are_ref[...], A_RE_MAX)
    li = aim_ref[...]
    dt = jnp.exp(ldt_ref[...])
    mag = jnp.exp(lr * dt)
    ar = mag * jnp.cos(li * dt)
    ai = mag * jnp.sin(li * dt)
    den = lr * lr + li * li
    fr = ((ar - 1.0) * lr + ai * li) / den
    fi = (ai * lr - (ar - 1.0) * li) / den
    br = bre_ref[...]
    bi = bim_ref[...]
    bbr_ref[...] = fr * br - fi * bi
    bbi_ref[...] = fr * bi + fi * br
    pr, pi = ar, ai
    pwr_ref[0:1, :] = pr
    pwi_ref[0:1, :] = pi
    for t in range(1, SUBLANES):
        pr, pi = pr * ar - pi * ai, pr * ai + pi * ar
        pwr_ref[t:t + 1, :] = pr
        pwi_ref[t:t + 1, :] = pi


def _s5_params(a_re, a_im, log_dt_b, b_re_t, b_im_t):
    return pl.pallas_call(
        _s5_params_kernel,
        out_shape=[
            jax.ShapeDtypeStruct((SUBLANES, N_STATE), F32),
            jax.ShapeDtypeStruct((SUBLANES, N_STATE), F32),
            jax.ShapeDtypeStruct((SSM_GROUP, N_STATE), F32),
            jax.ShapeDtypeStruct((SSM_GROUP, N_STATE), F32),
        ],
        name="s5_params",
    )(a_re, a_im, log_dt_b, b_re_t, b_im_t)


def _s5_matrices(bbr_t, bbi_t, c_re, c_im):
    G, P, C = SSM_GROUPS, SSM_STATE, SSM_GROUP
    bb = jnp.stack([bbr_t, bbi_t]).reshape(2, C, G, P)
    j = np.arange(STATE_TILES)
    g_of = (2 * j[:, None] + np.arange(2)[None, :])
    v = bb[:, :, g_of, :]
    v = jnp.transpose(v, (2, 1, 0, 3, 4))
    gl = np.arange(8)
    mask_b = (8 * (j[:, None, None] // 4) + gl[None, :, None] == g_of[:, None, :])
    bmat = (jnp.asarray(mask_b, F32)[:, :, None, None, :, None] * v[:, None])
    bmat = bmat.reshape(STATE_TILES, 8 * C, 2 * 2 * P).astype(BF16)

    cc = jnp.stack([c_re, -c_im])
    w = cc[:, g_of, :, :]
    w = jnp.transpose(w, (1, 0, 2, 4, 3))
    gl16 = np.arange(16)
    mask_c = (16 * (j[:, None, None] // 8) + gl16[None, None, :] == g_of[:, :, None])
    cmat = (w[:, :, :, :, None, :] * jnp.asarray(mask_c, F32)[:, None, :, None, :, None])
    cmat = cmat.reshape(STATE_TILES, 2 * 2 * P, 16 * C).astype(BF16)
    return bmat, cmat


def _s5_scan_kernel(um_ref, ux_ref, bmat_ref, cmat_ref, pwr_ref, pwi_ref, d_ref,
                    y_ref, s_ref, cr_ref, ci_ref, *, tc, unroll):
    i = pl.program_id(1)

    @pl.when(i == 0)
    def _():
        cr_ref[...] = jnp.zeros_like(cr_ref)
        ci_ref[...] = jnp.zeros_like(ci_ref)

    u = jnp.where(i == 0, um_ref[...], ux_ref[0])
    ub = u.astype(BF16)
    row = lax.broadcasted_iota(jnp.int32, (SUBLANES, LANES), 0)

    for j in range(STATE_TILES):
        c0 = 2 * LANES * j
        k0 = LANES * (j // 4)
        s_ref[:, c0:c0 + 2 * LANES] = jnp.dot(ub[:, k0:k0 + LANES], bmat_ref[j],
                                              preferred_element_type=F32)
        pr = pwr_ref[:, LANES * j:LANES * (j + 1)]
        pi = pwi_ref[:, LANES * j:LANES * (j + 1)]
        coef = []
        for d in (1, 2, 4):
            m = row >= d
            coef.append((jnp.where(m, jnp.broadcast_to(pr[d - 1:d, :], (SUBLANES, LANES)), 0.0),
                         jnp.where(m, jnp.broadcast_to(pi[d - 1:d, :], (SUBLANES, LANES)), 0.0)))

        def group(r, carry, c0=c0, pr=pr, pi=pi, coef=coef):
            cre, cim = carry
            o = pl.multiple_of(r * SUBLANES, SUBLANES)
            xr = s_ref[pl.ds(o, SUBLANES), c0:c0 + LANES]
            xi = s_ref[pl.ds(o, SUBLANES), c0 + LANES:c0 + 2 * LANES]
            for d, (kr, ki) in zip((1, 2, 4), coef):
                sr = pltpu.roll(xr, d, 0)
                si = pltpu.roll(xi, d, 0)
                xr, xi = xr + (kr * sr - ki * si), xi + (kr * si + ki * sr)
            xr = xr + (pr * cre - pi * cim)
            xi = xi + (pr * cim + pi * cre)
            s_ref[pl.ds(o, SUBLANES), c0:c0 + LANES] = xr
            s_ref[pl.ds(o, SUBLANES), c0 + LANES:c0 + 2 * LANES] = xi
            return (jnp.broadcast_to(xr[SUBLANES - 1:SUBLANES, :], (SUBLANES, LANES)),
                    jnp.broadcast_to(xi[SUBLANES - 1:SUBLANES, :], (SUBLANES, LANES)))

        def chunk(rr, carry, group=group):
            for q in range(unroll):
                carry = group(rr * unroll + q, carry)
            return carry

        cre, cim = lax.fori_loop(
            0, tc // (SUBLANES * unroll), chunk,
            (cr_ref[:, LANES * j:LANES * (j + 1)], ci_ref[:, LANES * j:LANES * (j + 1)]))
        cr_ref[:, LANES * j:LANES * (j + 1)] = cre
        ci_ref[:, LANES * j:LANES * (j + 1)] = cim

    half = STATE_TILES // 2
    for n in range(2):
        acc = jnp.zeros((tc, 2 * LANES), F32)
        for j in range(half * n, half * (n + 1)):
            c0 = 2 * LANES * j
            acc += jnp.dot(s_ref[:, c0:c0 + 2 * LANES].astype(BF16), cmat_ref[j],
                           preferred_element_type=F32)
        cols = slice(2 * LANES * n, 2 * LANES * (n + 1))
        y_ref[0, :, cols] = acc + d_ref[:, cols] * u[:, cols]


def _s5_scan(u_meta_tile, u_x, bmat, cmat, pw_re, pw_im, d_row, tc, unroll=4):
    B, L, _ = u_x.shape
    nt = L // tc
    xmap = lambda b, i: (b, jnp.maximum(i - 1, 0), 0)
    fixed2 = lambda b, i: (0, 0)
    fixed3 = lambda b, i: (0, 0, 0)
    return pl.pallas_call(
        functools.partial(_s5_scan_kernel, tc=tc, unroll=unroll),
        grid=(B, nt + 1),
        in_specs=[
            pl.BlockSpec((tc, SSM_WIDTH), fixed2),
            pl.BlockSpec((1, tc, SSM_WIDTH), xmap),
            pl.BlockSpec(bmat.shape, fixed3),
            pl.BlockSpec(cmat.shape, fixed3),
            pl.BlockSpec((SUBLANES, N_STATE), fixed2),
            pl.BlockSpec((SUBLANES, N_STATE), fixed2),
            pl.BlockSpec((1, SSM_WIDTH), fixed2),
        ],
        out_specs=pl.BlockSpec((1, tc, SSM_WIDTH), xmap),
        out_shape=jax.ShapeDtypeStruct((B, L, SSM_WIDTH), F32),
        scratch_shapes=[
            pltpu.VMEM((tc, 2 * N_STATE), F32),
            pltpu.VMEM((SUBLANES, N_STATE), F32),
            pltpu.VMEM((SUBLANES, N_STATE), F32),
        ],
        compiler_params=_cparams("arbitrary", "arbitrary"),
        name="s5_scan",
    )(u_meta_tile, u_x, bmat, cmat, pw_re, pw_im, d_row)


def _attn_kernel(lam_ref, q_ref, k_ref, v_ref, km_ref, vm_ref, g_ref, o_ref,
                 m_ref, l_ref, acc_ref, *, tq):
    qi = pl.program_id(2)
    q = q_ref[0]
    lane = lax.broadcasted_iota(jnp.int32, q.shape, 1)
    zero = jnp.zeros_like(q)
    qm = (jnp.where(lane < HEAD_DIM, q, zero), jnp.where(lane >= HEAD_DIM, q, zero))

    m_ref[...] = jnp.full_like(m_ref, NEG_INF)
    l_ref[...] = jnp.zeros_like(l_ref)
    acc_ref[...] = jnp.zeros_like(acc_ref)

    def update(kblk, vblk, mask):
        for mi in range(2):
            s = lax.dot_general(qm[mi], kblk, (((1,), (1,)), ((), ())),
                                preferred_element_type=F32)
            if mask is not None:
                s = jnp.where(mask, s, NEG_INF)
            m_old = m_ref[mi]
            m_new = jnp.maximum(m_old, jnp.max(s, axis=-1, keepdims=True))
            alpha = jnp.exp(m_old - m_new)
            p = jnp.exp(s - m_new)
            l_ref[mi] = alpha * l_ref[mi] + jnp.sum(p, axis=-1, keepdims=True)
            acc_ref[mi] = alpha * acc_ref[mi] + jnp.dot(p.astype(BF16), vblk,
                                                        preferred_element_type=F32)
            m_ref[mi] = m_new

    col_m = lax.broadcasted_iota(jnp.int32, (tq, META_ROWS), 1)
    update(km_ref[...], vm_ref[...], col_m < N_META)

    def full_block(kb, carry):
        o = pl.multiple_of(kb * tq, tq)
        update(k_ref[0, pl.ds(o, tq), :], v_ref[0, pl.ds(o, tq), :], None)
        return carry

    lax.fori_loop(0, qi, full_block, 0)

    o = pl.multiple_of(qi * tq, tq)
    r_i = lax.broadcasted_iota(jnp.int32, (tq, tq), 0)
    c_i = lax.broadcasted_iota(jnp.int32, (tq, tq), 1)
    update(k_ref[0, pl.ds(o, tq), :], v_ref[0, pl.ds(o, tq), :], c_i <= r_i)

    lam = lam_ref[0]
    out = acc_ref[0] / l_ref[0] - lam * (acc_ref[1] / l_ref[1])
    ms = jnp.mean(out * out, axis=-1, keepdims=True)
    o_ref[0] = (out * lax.rsqrt(ms + EPS) * g_ref[...]).astype(o_ref.dtype)


def _diff_attn(lam, q, k, v, k_meta, v_meta, sub_g, tq):
    B, L, _ = q.shape
    hw = 2 * HEAD_DIM
    return pl.pallas_call(
        functools.partial(_attn_kernel, tq=tq),
        grid=(B, ATTN_HEADS, L // tq),
        in_specs=[
            pl.BlockSpec(memory_space=pltpu.SMEM),
            pl.BlockSpec((1, tq, hw), lambda b, h, i: (b, i, h)),
            pl.BlockSpec((1, L, hw), lambda b, h, i: (b, 0, h)),
            pl.BlockSpec((1, L, hw), lambda b, h, i: (b, 0, h)),
            pl.BlockSpec((META_ROWS, hw), lambda b, h, i: (0, h)),
            pl.BlockSpec((META_ROWS, hw), lambda b, h, i: (0, h)),
            pl.BlockSpec((1, hw), lambda b, h, i: (0, 0)),
        ],
        out_specs=pl.BlockSpec((1, tq, hw), lambda b, h, i: (b, i, h)),
        out_shape=jax.ShapeDtypeStruct((B, L, ATTN_WIDTH), BF16),
        scratch_shapes=[
            pltpu.VMEM((2, tq, 1), F32),
            pltpu.VMEM((2, tq, 1), F32),
            pltpu.VMEM((2, tq, hw), F32),
        ],
        compiler_params=_cparams("arbitrary", "arbitrary", "arbitrary"),
        name="diff_attn",
    )(lam, q, k, v, k_meta, v_meta, sub_g)


def _split_bf16(x):
    hi = x.astype(BF16)
    lo = (x - hi.astype(F32)).astype(BF16)
    return hi, lo


def _merge_kernel(x_ref, ys_ref, ya_ref, gmix_ref, wg_ref, wglu_ref, bglu_ref, wso_ref, wao_ref,
                  wo_ref, gffn_ref, wrh_ref, wrl_ref, br_ref, h_ref, xn_ref, lg_ref):
    x = x_ref[...]
    ms = jnp.mean(x * x, axis=-1, keepdims=True)
    n = (x * lax.rsqrt(ms + EPS) * gmix_ref[...]).astype(BF16)
    gates = jax.nn.sigmoid(jnp.dot(n, wg_ref[...], preferred_element_type=F32))

    ys = jax.nn.gelu(ys_ref[...])
    ys = ys * jax.nn.sigmoid(jnp.dot(ys.astype(BF16), wglu_ref[...], preferred_element_type=F32)
                             + bglu_ref[...])
    ys = jnp.dot(ys.astype(BF16), wso_ref[...], preferred_element_type=F32)
    ya = jnp.dot(ya_ref[...], wao_ref[...], preferred_element_type=F32)
    mixed = gates[:, :D_MODEL] * ys + gates[:, D_MODEL:] * ya
    h = x + jnp.dot(mixed.astype(BF16), wo_ref[...], preferred_element_type=F32)
    h_ref[...] = h

    ms2 = jnp.mean(h * h, axis=-1, keepdims=True)
    xn = h * lax.rsqrt(ms2 + EPS) * gffn_ref[...]
    xn_ref[...] = xn
    xh, xl = _split_bf16(xn)
    nt = (((1,), (1,)), ((), ()))
    lg = (lax.dot_general(wrh_ref[...], xh, nt, preferred_element_type=F32)
          + lax.dot_general(wrh_ref[...], xl, nt, preferred_element_type=F32)
          + lax.dot_general(wrl_ref[...], xh, nt, preferred_element_type=F32))
    lg_ref[...] = lg + br_ref[...]


def _merge(x2, ys, ya, gmix, wg, wglu, bglu, wso, wao, wo, gffn, wrh, wrl, br, tm):
    T = x2.shape[0]
    row = lambda i: (i, 0)
    fixed = lambda i: (0, 0)
    full = lambda a: pl.BlockSpec(a.shape, fixed)
    return pl.pallas_call(
        _merge_kernel,
        grid=(T // tm,),
        in_specs=[
            pl.BlockSpec((tm, D_MODEL), row),
            pl.BlockSpec((tm, SSM_WIDTH), row),
            pl.BlockSpec((tm, ATTN_WIDTH), row),
            full(gmix), full(wg), full(wglu), full(bglu), full(wso), full(wao), full(wo),
            full(gffn), full(wrh), full(wrl), full(br),
        ],
        out_specs=[
            pl.BlockSpec((tm, D_MODEL), row),
            pl.BlockSpec((tm, D_MODEL), row),
            pl.BlockSpec((N_EXPERTS, tm), lambda i: (0, i)),
        ],
        out_shape=[
            jax.ShapeDtypeStruct((T, D_MODEL), F32),
            jax.ShapeDtypeStruct((T, D_MODEL), F32),
            jax.ShapeDtypeStruct((N_EXPERTS, T), F32),
        ],
        compiler_params=_cparams("arbitrary"),
        name="merge",
    )(x2, ys, ya, gmix, wg, wglu, bglu, wso, wao, wo, gffn, wrh, wrl, br)


def _route_kernel(lg_ref, tri_ref, idx_ref, gate_ref, rank_ref, cnt_ref, base_ref):
    i = pl.program_id(0)

    @pl.when(i == 0)
    def _():
        base_ref[...] = jnp.zeros_like(base_ref)

    work = lg_ref[...]
    row = lax.broadcasted_iota(jnp.int32, work.shape, 0).astype(F32)
    vals, idxs, onehots = [], [], []
    for _ in range(TOP_K):
        mx = jnp.max(work, axis=0, keepdims=True)
        ek = jnp.min(jnp.where(work == mx, row, float(N_EXPERTS)), axis=0, keepdims=True)
        oh = row == ek
        work = jnp.where(oh, -jnp.inf, work)
        vals.append(mx)
        idxs.append(ek)
        onehots.append(oh)
    ex = [jnp.exp(v - vals[0]) for v in vals]
    den = ex[0] + ex[1] + ex[2] + ex[3]
    cnt = jnp.zeros(work.shape, F32)
    for oh in onehots:
        cnt = cnt + oh.astype(F32)
    pre = jnp.dot(cnt.astype(BF16), tri_ref[...], preferred_element_type=F32) + base_ref[:, 0:1]
    for kk in range(TOP_K):
        idx_ref[kk:kk + 1, :] = idxs[kk].astype(jnp.int32)
        gate_ref[kk:kk + 1, :] = ex[kk] / den
        rank_ref[kk:kk + 1, :] = jnp.sum(jnp.where(onehots[kk], pre, 0.0), axis=0,
                                         keepdims=True).astype(jnp.int32)
    base_ref[...] = base_ref[...] + jnp.sum(cnt, axis=1, keepdims=True)
    cnt_ref[...] = base_ref[...]


def _route(logits_t, tt):
    T = logits_t.shape[1]
    tri = jnp.asarray(np.triu(np.ones((tt, tt), np.float32), k=1), BF16)
    col = lambda i: (0, i)
    return pl.pallas_call(
        _route_kernel,
        grid=(T // tt,),
        in_specs=[pl.BlockSpec((N_EXPERTS, tt), col), pl.BlockSpec((tt, tt), lambda i: (0, 0))],
        out_specs=[
            pl.BlockSpec((TOP_K, tt), col),
            pl.BlockSpec((TOP_K, tt), col),
            pl.BlockSpec((TOP_K, tt), col),
            pl.BlockSpec((N_EXPERTS, LANES), lambda i: (0, 0)),
        ],
        out_shape=[
            jax.ShapeDtypeStruct((TOP_K, T), jnp.int32),
            jax.ShapeDtypeStruct((TOP_K, T), F32),
            jax.ShapeDtypeStruct((TOP_K, T), jnp.int32),
            jax.ShapeDtypeStruct((N_EXPERTS, LANES), F32),
        ],
        scratch_shapes=[pltpu.VMEM((N_EXPERTS, LANES), F32)],
        compiler_params=_cparams("arbitrary"),
        name="route",
    )(logits_t, tri)


def _experts_kernel(blke_ref, nblk_ref, tok_ref, slot_ref, gate_ref, xn_hbm,
                    wg_ref, bg_ref, wu_ref, bu_ref, wd_ref, bd_ref, ya_hbm,
                    xbuf, ybuf, wgb, wub, wdb, gsem, ssem, *, bm):
    i = pl.program_id(0)
    nb = pl.num_programs(0)
    slot_now = i % 2

    def gather_rows(which, slot):
        def issue(r, c):
            t = tok_ref[which, 0, r]
            pltpu.make_async_copy(xn_hbm.at[pl.ds(t, 1), :], xbuf.at[slot, pl.ds(r, 1), :],
                                  gsem.at[slot]).start()
            return c
        lax.fori_loop(0, bm, issue, 0)

    def wait_gather(slot):
        pltpu.make_async_copy(xn_hbm.at[pl.ds(0, bm), :], xbuf.at[slot], gsem.at[slot]).wait()

    def wait_scatter(slot):
        pltpu.make_async_copy(ybuf.at[slot], ya_hbm.at[pl.ds(0, bm), :], ssem.at[slot]).wait()

    @pl.when(i == 0)
    def _():
        gather_rows(0, 0)

    @pl.when(i + 1 < nb)
    def _():
        gather_rows(1, 1 - slot_now)

    changed = jnp.logical_or(i == 0, blke_ref[i] != blke_ref[jnp.maximum(i - 1, 0)])

    @pl.when(changed)
    def _():
        wgb[...] = wg_ref[0].astype(BF16)
        wub[...] = wu_ref[0].astype(BF16)
        wdb[...] = wd_ref[0].astype(BF16)

    wait_gather(slot_now)

    @pl.when(i >= 2)
    def _():
        wait_scatter(slot_now)

    @pl.when(i < nblk_ref[0])
    def _():
        hb = xbuf[slot_now].astype(BF16)
        g = jnp.minimum(jnp.dot(hb, wgb[...], preferred_element_type=F32) + bg_ref[0], SWIGLU_LIMIT)
        u = jnp.clip(jnp.dot(hb, wub[...], preferred_element_type=F32) + bu_ref[0],
                     -SWIGLU_LIMIT, SWIGLU_LIMIT)
        a = (u + 1.0) * (g * jax.nn.sigmoid(SWIGLU_ALPHA * g))
        y = jnp.dot(a.astype(BF16), wdb[...], preferred_element_type=F32) + bd_ref[0]
        ybuf[slot_now] = y * gate_ref[0]

    @pl.when(i >= nblk_ref[0])
    def _():
        ybuf[slot_now] = jnp.zeros((bm, D_MODEL), F32)

    def scatter(r, c):
        s = slot_ref[0, 0, r]
        pltpu.make_async_copy(ybuf.at[slot_now, pl.ds(r, 1), :], ya_hbm.at[pl.ds(s, 1), :],
                              ssem.at[slot_now]).start()
        return c
    lax.fori_loop(0, bm, scatter, 0)

    @pl.when(i == nb - 1)
    def _():
        wait_scatter(slot_now)

        @pl.when(nb >= 2)
        def _():
            wait_scatter(1 - slot_now)


def _experts(blk_e, n_used, row_tok2, row_slot, row_gate, xn, w_gate, b_gate, w_up, b_up,
             w_down, b_down, n_slots, bm):
    n_blocks = row_slot.shape[0]
    E, D, Fd = w_gate.shape
    wmap = lambda i, be, nu: (be[i], 0, 0)
    smem_blk = lambda shape, imap: pl.BlockSpec(shape, imap, memory_space=pltpu.SMEM)
    grid_spec = pltpu.PrefetchScalarGridSpec(
        num_scalar_prefetch=2,
        grid=(n_blocks,),
        in_specs=[
            smem_blk((2, 1, bm), lambda i, be, nu: (i, 0, 0)),
            smem_blk((1, 1, bm), lambda i, be, nu: (i, 0, 0)),
            pl.BlockSpec((1, bm, 1), lambda i, be, nu: (i, 0, 0)),
            pl.BlockSpec(memory_space=pl.ANY),
            pl.BlockSpec((1, D, Fd), wmap),
            pl.BlockSpec((1, 1, Fd), wmap),
            pl.BlockSpec((1, D, Fd), wmap),
            pl.BlockSpec((1, 1, Fd), wmap),
            pl.BlockSpec((1, Fd, D), wmap),
            pl.BlockSpec((1, 1, D), wmap),
        ],
        out_specs=pl.BlockSpec(memory_space=pl.ANY),
        scratch_shapes=[
            pltpu.VMEM((2, bm, D), F32),
            pltpu.VMEM((2, bm, D), F32),
            pltpu.VMEM((D, Fd), BF16),
            pltpu.VMEM((D, Fd), BF16),
            pltpu.VMEM((Fd, D), BF16),
            pltpu.SemaphoreType.DMA((2,)),
            pltpu.SemaphoreType.DMA((2,)),
        ],
    )
    return pl.pallas_call(
        functools.partial(_experts_kernel, bm=bm),
        grid_spec=grid_spec,
        out_shape=jax.ShapeDtypeStruct((n_slots, D), F32),
        compiler_params=_cparams("arbitrary"),
        name="experts",
    )(blk_e, n_used, row_tok2, row_slot, row_gate, xn, w_gate, b_gate, w_up, b_up, w_down, b_down)


def _combine_kernel(h_ref, y_ref, o_ref):
    y = y_ref[...]
    acc = h_ref[...]
    for kk in range(TOP_K):
        acc = acc + y[:, kk * D_MODEL:(kk + 1) * D_MODEL]
    o_ref[...] = acc


def _combine(h2, y4, tm):
    T = h2.shape[0]
    return pl.pallas_call(
        _combine_kernel,
        grid=(T // tm,),
        in_specs=[pl.BlockSpec((tm, D_MODEL), lambda i: (i, 0)),
                  pl.BlockSpec((tm, TOP_K * D_MODEL), lambda i: (i, 0))],
        out_specs=pl.BlockSpec((tm, D_MODEL), lambda i: (i, 0)),
        out_shape=jax.ShapeDtypeStruct((T, D_MODEL), F32),
        compiler_params=_cparams("arbitrary"),
        name="combine",
    )(h2, y4)


def _routing_tables(idx, gate, rank, counts, T, bm):
    A = T * TOP_K
    n_blocks = A // bm + N_EXPERTS
    n_rows = n_blocks * bm
    counts = counts.astype(jnp.int32)
    padded = (counts + bm - 1) // bm * bm
    pend = jnp.cumsum(padded)
    pstart = pend - padded
    dest = (pstart[idx] + rank).reshape(A)
    tok = jnp.broadcast_to(jnp.arange(T, dtype=jnp.int32)[None, :], (TOP_K, T))
    slot = (tok * TOP_K + jnp.arange(TOP_K, dtype=jnp.int32)[:, None]).reshape(A)
    tok = tok.reshape(A)
    row_tok = jnp.zeros((n_rows,), jnp.int32).at[dest].set(tok)
    is_pad = jnp.ones((n_rows,), jnp.int32).at[dest].set(0)
    dump = A - 1 + jnp.cumsum(is_pad)
    row_slot = dump.astype(jnp.int32).at[dest].set(slot)
    row_gate = jnp.zeros((n_rows,), F32).at[dest].set(gate.reshape(A))
    blk_e = jnp.minimum(jnp.searchsorted(pend, jnp.arange(n_blocks, dtype=jnp.int32) * bm, side='right'),
                        N_EXPERTS - 1).astype(jnp.int32)
    n_used = (pend[-1] // bm).astype(jnp.int32).reshape(1)
    tok_b = row_tok.reshape(n_blocks, 1, bm)
    tok_next = jnp.concatenate([tok_b[1:], tok_b[-1:]], axis=0)
    row_tok2 = jnp.stack([tok_b, tok_next], axis=1).reshape(2 * n_blocks, 1, bm)
    return (blk_e, n_used, row_tok2, row_slot.reshape(n_blocks, 1, bm),
            row_gate.reshape(n_blocks, bm, 1), n_rows)


def _layer(x, meta_tokens, norm_mix_g, w_in, ssm_a_re, ssm_a_im, ssm_log_dt, ssm_b_re, ssm_b_im,
           ssm_c_re, ssm_c_im, ssm_d, w_glu, b_glu, q_norm_g, k_norm_g, lambda_q1, lambda_k1,
           lambda_q2, lambda_k2, subln_g, w_ssm_out, w_attn_out, w_o, norm_ffn_g, w_router,
           b_router, w_gate, b_gate, w_up, b_up, w_down, b_down,
           *, tm=512, tc=256, tq=512, tt=512, bm=256):
    B, L, D = x.shape
    T = B * L
    G, P, C = SSM_GROUPS, SSM_STATE, SSM_GROUP
    o_g = SSM_WIDTH + 3 * ATTN_WIDTH

    w_uqkv = w_in[:, :o_g].astype(BF16)
    w_gates = w_in[:, o_g:].astype(BF16)
    gmix = norm_mix_g.reshape(1, D)
    scale = HEAD_DIM ** -0.5
    qg = (jnp.tile(q_norm_g, 2 * ATTN_HEADS) * scale).reshape(1, ATTN_WIDTH)
    kg = jnp.tile(k_norm_g, 2 * ATTN_HEADS).reshape(1, ATTN_WIDTH)
    gmat = jnp.asarray(np.kron(np.eye(2 * ATTN_HEADS), np.ones((HEAD_DIM, HEAD_DIM))) / HEAD_DIM, BF16)

    x2 = x.reshape(T, D)
    u_x, q_x, k_x, v_x = _in_proj(x2, gmix, w_uqkv, qg, kg, gmat, tm)
    meta_pad = jnp.zeros((META_ROWS, D), F32).at[:N_META].set(meta_tokens.astype(F32))
    u_m, _, k_m, v_m = _in_proj(meta_pad, gmix, w_uqkv, qg, kg, gmat, META_ROWS)

    pw_re, pw_im, bbr_t, bbi_t = _s5_params(
        ssm_a_re.reshape(1, G * P), ssm_a_im.reshape(1, G * P),
        jnp.repeat(ssm_log_dt, P).reshape(1, G * P),
        jnp.transpose(ssm_b_re, (2, 0, 1)).reshape(C, G * P),
        jnp.transpose(ssm_b_im, (2, 0, 1)).reshape(C, G * P))
    bmat, cmat = _s5_matrices(bbr_t, bbi_t, ssm_c_re, ssm_c_im)
    u_meta_tile = jnp.zeros((tc, SSM_WIDTH), F32).at[tc - N_META:].set(u_m[:N_META])
    ys = _s5_scan(u_meta_tile, u_x.reshape(B, L, SSM_WIDTH), bmat, cmat, pw_re, pw_im,
                  ssm_d.reshape(1, SSM_WIDTH), tc)

    lam = (jnp.exp(jnp.sum(lambda_q1 * lambda_k1)) - jnp.exp(jnp.sum(lambda_q2 * lambda_k2))
           + LAMBDA_INIT).reshape(1).astype(F32)
    sub_g = (subln_g * (1.0 - LAMBDA_INIT)).reshape(1, 2 * HEAD_DIM)
    ya = _diff_attn(lam, q_x.reshape(B, L, ATTN_WIDTH), k_x.reshape(B, L, ATTN_WIDTH),
                    v_x.reshape(B, L, ATTN_WIDTH), k_m, v_m, sub_g, tq)

    wrh, wrl = _split_bf16(jnp.transpose(w_router))
    h2, xn, logits_t = _merge(
        x2, ys.reshape(T, SSM_WIDTH), ya.reshape(T, ATTN_WIDTH), gmix, w_gates,
        w_glu.astype(BF16), b_glu.reshape(1, SSM_WIDTH), w_ssm_out.astype(BF16),
        w_attn_out.astype(BF16), w_o.astype(BF16), norm_ffn_g.reshape(1, D),
        wrh, wrl, b_router.reshape(N_EXPERTS, 1), tm)

    idx, gate, rank, cnt = _route(logits_t, tt)
    blk_e, n_used, row_tok2, row_slot, row_gate, n_slots = _routing_tables(
        idx, gate, rank, cnt[:, 0], T, bm)
    y_assign = _experts(blk_e, n_used, row_tok2, row_slot, row_gate, xn,
                        w_gate, b_gate.reshape(N_EXPERTS, 1, -1), w_up, b_up.reshape(N_EXPERTS, 1, -1),
                        w_down, b_down.reshape(N_EXPERTS, 1, -1), n_slots, bm)
    out = _combine(h2, y_assign.reshape(n_slots // TOP_K, TOP_K * D), tm)
    return out.reshape(B, L, D)


def kernel(x, meta_tokens, norm_mix_g, w_in, ssm_a_re, ssm_a_im, ssm_log_dt, ssm_b_re, ssm_b_im,
           ssm_c_re, ssm_c_im, ssm_d, w_glu, b_glu, q_norm_g, k_norm_g, lambda_q1, lambda_k1,
           lambda_q2, lambda_k2, subln_g, w_ssm_out, w_attn_out, w_o, norm_ffn_g, w_router,
           b_router, w_gate, b_gate, w_up, b_up, w_down, b_down):
    p = [a[0] for a in (norm_mix_g, w_in, ssm_a_re, ssm_a_im, ssm_log_dt, ssm_b_re, ssm_b_im,
                        ssm_c_re, ssm_c_im, ssm_d, w_glu, b_glu, q_norm_g, k_norm_g, lambda_q1,
                        lambda_k1, lambda_q2, lambda_k2, subln_g, w_ssm_out, w_attn_out, w_o,
                        norm_ffn_g, w_router, b_router, w_gate, b_gate, w_up, b_up, w_down, b_down)]
    return _layer(x, meta_tokens, *p)
```

```python
import functools
import math

import numpy as np
import jax
import jax.numpy as jnp
from jax import lax
from jax.experimental import pallas as pl
from jax.experimental.pallas import tpu as pltpu

F32 = jnp.float32
BF16 = jnp.bfloat16
I32 = jnp.int32

D_MODEL = 1024
N_META = 16
SSM_WIDTH = 512
SSM_GROUP = 16
SSM_GROUPS = 32
SSM_STATE = 64
A_RE_MAX = -1e-4
ATTN_HEADS = 4
HEAD_DIM = 64
ATTN_WIDTH = 512
NEG_INF = -1e30
N_EXPERTS = 32
TOP_K = 4
SWIGLU_LIMIT = 7.0
SWIGLU_ALPHA = 1.702
EPS = 1e-6
LAMBDA_INIT = 0.8 - 0.6 * math.exp(-0.3 * 0)

LANES = 128
SUBLANES = 8
META_ROWS = 128
VMEM_LIMIT = 56 * 1024 * 1024
DMA_UNROLL = 8

N_STATE = SSM_GROUPS * SSM_STATE
STATE_TILES = N_STATE // LANES
HEAD_W = 2 * HEAD_DIM


def _cparams(*sem):
    return pltpu.CompilerParams(dimension_semantics=sem, vmem_limit_bytes=VMEM_LIMIT)


def _in_proj_kernel(x_ref, g_ref, w_ref, qg_ref, kg_ref, gm_ref, u_ref, qt_ref, k_ref, vt_ref):
    x = x_ref[...]
    ms = jnp.mean(x * x, axis=-1, keepdims=True)
    n = (x * lax.rsqrt(ms + EPS) * g_ref[...]).astype(BF16)
    z = jnp.dot(n, w_ref[...], preferred_element_type=F32)
    u_ref[...] = z[:, :SSM_WIDTH]
    gm = gm_ref[...]

    def head_norm(t, g):
        ms_g = jnp.dot((t * t).astype(BF16), gm, preferred_element_type=F32)
        return t * lax.rsqrt(ms_g + EPS) * g

    q = z[:, SSM_WIDTH:SSM_WIDTH + ATTN_WIDTH]
    k = z[:, SSM_WIDTH + ATTN_WIDTH:SSM_WIDTH + 2 * ATTN_WIDTH]
    qt_ref[...] = jnp.transpose(head_norm(q, qg_ref[...])).astype(BF16)
    k_ref[...] = head_norm(k, kg_ref[...]).astype(BF16)
    vt_ref[...] = jnp.transpose(z[:, SSM_WIDTH + 2 * ATTN_WIDTH:]).astype(BF16)


def _in_proj(x2, g_mix, w_uqkv, qg, kg, gmat, tm):
    n_rows = x2.shape[0]
    wdt = w_uqkv.shape[1]
    row = lambda i: (i, 0)
    col = lambda i: (0, i)
    fixed = lambda i: (0, 0)
    return pl.pallas_call(
        _in_proj_kernel,
        grid=(n_rows // tm,),
        in_specs=[
            pl.BlockSpec((tm, D_MODEL), row),
            pl.BlockSpec((1, D_MODEL), fixed),
            pl.BlockSpec((D_MODEL, wdt), fixed),
            pl.BlockSpec((1, ATTN_WIDTH), fixed),
            pl.BlockSpec((1, ATTN_WIDTH), fixed),
            pl.BlockSpec((ATTN_WIDTH, ATTN_WIDTH), fixed),
        ],
        out_specs=[
            pl.BlockSpec((tm, SSM_WIDTH), row),
            pl.BlockSpec((ATTN_WIDTH, tm), col),
            pl.BlockSpec((tm, ATTN_WIDTH), row),
            pl.BlockSpec((ATTN_WIDTH, tm), col),
        ],
        out_shape=[
            jax.ShapeDtypeStruct((n_rows, SSM_WIDTH), F32),
            jax.ShapeDtypeStruct((ATTN_WIDTH, n_rows), BF16),
            jax.ShapeDtypeStruct((n_rows, ATTN_WIDTH), BF16),
            jax.ShapeDtypeStruct((ATTN_WIDTH, n_rows), BF16),
        ],
        compiler_params=_cparams("arbitrary"),
        name="in_proj",
    )(x2, g_mix, w_uqkv, qg, kg, gmat)


def _s5_params_kernel(are_ref, aim_ref, ldt_ref, bre_ref, bim_ref,
                      pwr_ref, pwi_ref, bbr_ref, bbi_ref):
    lr = jnp.minimum(are_ref[...], A_RE_MAX)
    li = aim_ref[...]
    dt = jnp.exp(ldt_ref[...])
    mag = jnp.exp(lr * dt)
    ar = mag * jnp.cos(li * dt)
    ai = mag * jnp.sin(li * dt)
    den = lr * lr + li * li
    fr = ((ar - 1.0) * lr + ai * li) / den
    fi = (ai * lr - (ar - 1.0) * li) / den
    br = bre_ref[...]
    bi = bim_ref[...]
    bbr_ref[...] = fr * br - fi * bi
    bbi_ref[...] = fr * bi + fi * br
    pr, pi = ar, ai
    pwr_ref[0:1, :] = pr
    pwi_ref[0:1, :] = pi
    for t in range(1, SUBLANES):
        pr, pi = pr * ar - pi * ai, pr * ai + pi * ar
        pwr_ref[t:t + 1, :] = pr
        pwi_ref[t:t + 1, :] = pi


def _s5_params(a_re, a_im, log_dt_b, b_re_t, b_im_t):
    return pl.pallas_call(
        _s5_params_kernel,
        out_shape=[
            jax.ShapeDtypeStruct((SUBLANES, N_STATE), F32),
            jax.ShapeDtypeStruct((SUBLANES, N_STATE), F32),
            jax.ShapeDtypeStruct((SSM_GROUP, N_STATE), F32),
            jax.ShapeDtypeStruct((SSM_GROUP, N_STATE), F32),
        ],
        name="s5_params",
    )(a_re, a_im, log_dt_b, b_re_t, b_im_t)


def _s5_matrices(bbr_t, bbi_t, c_re, c_im):
    G, P, C = SSM_GROUPS, SSM_STATE, SSM_GROUP
    bb = jnp.stack([bbr_t, bbi_t]).reshape(2, C, G, P)
    j = np.arange(STATE_TILES)
    g_of = (2 * j[:, None] + np.arange(2)[None, :])
    v = bb[:, :, g_of, :]
    v = jnp.transpose(v, (2, 1, 0, 3, 4))
    gl = np.arange(8)
    mask_b = (8 * (j[:, None, None] // 4) + gl[None, :, None] == g_of[:, None, :])
    bmat = (jnp.asarray(mask_b, F32)[:, :, None, None, :, None] * v[:, None])
    bmat = bmat.reshape(STATE_TILES, 8 * C, 2 * 2 * P).astype(BF16)

    cc = jnp.stack([c_re, -c_im])
    w = cc[:, g_of, :, :]
    w = jnp.transpose(w, (1, 0, 2, 4, 3))
    gl16 = np.arange(16)
    mask_c = (16 * (j[:, None, None] // 8) + gl16[None, None, :] == g_of[:, :, None])
    cmat = (w[:, :, :, :, None, :] * jnp.asarray(mask_c, F32)[:, None, :, None, :, None])
    cmat = cmat.reshape(STATE_TILES, 2 * 2 * P, 16 * C).astype(BF16)
    return bmat, cmat


def _s5_scan_kernel(um_ref, ux_ref, bmat_ref, cmat_ref, pwr_ref, pwi_ref, d_ref,
                    y_ref, s_ref, cr_ref, ci_ref, *, tc, tiles_per_loop):
    i = pl.program_id(1)

    @pl.when(i == 0)
    def _():
        cr_ref[...] = jnp.zeros_like(cr_ref)
        ci_ref[...] = jnp.zeros_like(ci_ref)

    u = jnp.where(i == 0, um_ref[...], ux_ref[0])
    ub = u.astype(BF16)

    for j in range(STATE_TILES):
        c0 = 2 * LANES * j
        k0 = LANES * (j // 4)
        s_ref[:, c0:c0 + 2 * LANES] = jnp.dot(ub[:, k0:k0 + LANES], bmat_ref[j],
                                              preferred_element_type=F32)

    row = lax.broadcasted_iota(I32, (SUBLANES, LANES), 0)
    for j0 in range(0, STATE_TILES, tiles_per_loop):
        tiles = list(range(j0, j0 + tiles_per_loop))
        consts = []
        for j in tiles:
            lanes = slice(LANES * j, LANES * (j + 1))
            pr = pwr_ref[:, lanes]
            pi = pwi_ref[:, lanes]
            coef = []
            for d in (1, 2, 4):
                m = row >= d
                coef.append((jnp.where(m, jnp.broadcast_to(pr[d - 1:d, :], (SUBLANES, LANES)), 0.0),
                             jnp.where(m, jnp.broadcast_to(pi[d - 1:d, :], (SUBLANES, LANES)), 0.0)))
            consts.append((pr, pi, coef))

        def group(r, carry, tiles=tiles, consts=consts):
            o = pl.multiple_of(r * SUBLANES, SUBLANES)
            out = []
            for j, (pr, pi, coef), (cre, cim) in zip(tiles, consts, carry):
                c0 = 2 * LANES * j
                xr = s_ref[pl.ds(o, SUBLANES), c0:c0 + LANES]
                xi = s_ref[pl.ds(o, SUBLANES), c0 + LANES:c0 + 2 * LANES]
                for d, (kr, ki) in zip((1, 2, 4), coef):
                    sr = pltpu.roll(xr, d, 0)
                    si = pltpu.roll(xi, d, 0)
                    xr, xi = xr + (kr * sr - ki * si), xi + (kr * si + ki * sr)
                xr = xr + (pr * cre - pi * cim)
                xi = xi + (pr * cim + pi * cre)
                s_ref[pl.ds(o, SUBLANES), c0:c0 + LANES] = xr
                s_ref[pl.ds(o, SUBLANES), c0 + LANES:c0 + 2 * LANES] = xi
                out.append((jnp.broadcast_to(xr[SUBLANES - 1:SUBLANES, :], (SUBLANES, LANES)),
                            jnp.broadcast_to(xi[SUBLANES - 1:SUBLANES, :], (SUBLANES, LANES))))
            return tuple(out)

        init = tuple((cr_ref[:, LANES * j:LANES * (j + 1)], ci_ref[:, LANES * j:LANES * (j + 1)])
                     for j in tiles)
        fin = lax.fori_loop(0, tc // SUBLANES, group, init)
        for j, (cre, cim) in zip(tiles, fin):
            cr_ref[:, LANES * j:LANES * (j + 1)] = cre
            ci_ref[:, LANES * j:LANES * (j + 1)] = cim

    half = STATE_TILES // 2
    for n in range(2):
        acc = jnp.zeros((tc, 2 * LANES), F32)
        for j in range(half * n, half * (n + 1)):
            c0 = 2 * LANES * j
            acc += jnp.dot(s_ref[:, c0:c0 + 2 * LANES].astype(BF16), cmat_ref[j],
                           preferred_element_type=F32)
        cols = slice(2 * LANES * n, 2 * LANES * (n + 1))
        y_ref[0, :, cols] = acc + d_ref[:, cols] * u[:, cols]


def _s5_scan(u_meta_tile, u_x, bmat, cmat, pw_re, pw_im, d_row, tc, tiles_per_loop=4):
    B, L, _ = u_x.shape
    nt = L // tc
    xmap = lambda b, i: (b, jnp.maximum(i - 1, 0), 0)
    fixed2 = lambda b, i: (0, 0)
    fixed3 = lambda b, i: (0, 0, 0)
    return pl.pallas_call(
        functools.partial(_s5_scan_kernel, tc=tc, tiles_per_loop=tiles_per_loop),
        grid=(B, nt + 1),
        in_specs=[
            pl.BlockSpec((tc, SSM_WIDTH), fixed2),
            pl.BlockSpec((1, tc, SSM_WIDTH), xmap),
            pl.BlockSpec(bmat.shape, fixed3),
            pl.BlockSpec(cmat.shape, fixed3),
            pl.BlockSpec((SUBLANES, N_STATE), fixed2),
            pl.BlockSpec((SUBLANES, N_STATE), fixed2),
            pl.BlockSpec((1, SSM_WIDTH), fixed2),
        ],
        out_specs=pl.BlockSpec((1, tc, SSM_WIDTH), xmap),
        out_shape=jax.ShapeDtypeStruct((B, L, SSM_WIDTH), F32),
        scratch_shapes=[
            pltpu.VMEM((tc, 2 * N_STATE), F32),
            pltpu.VMEM((SUBLANES, N_STATE), F32),
            pltpu.VMEM((SUBLANES, N_STATE), F32),
        ],
        compiler_params=_cparams("arbitrary", "arbitrary"),
        name="s5_scan",
    )(u_meta_tile, u_x, bmat, cmat, pw_re, pw_im, d_row)


def _attn_kernel(lam_ref, qt_ref, k_ref, vt_ref, km_ref, vmt_ref, g_ref, o_ref, acc_ref, *, tq):
    qi = pl.program_id(2)
    qt = qt_ref[...]
    feat = lax.broadcasted_iota(I32, qt.shape, 0)
    zero = jnp.zeros_like(qt)
    qtm = (jnp.where(feat < HEAD_DIM, qt, zero), jnp.where(feat >= HEAD_DIM, qt, zero))
    acc_ref[...] = jnp.zeros_like(acc_ref)

    def update(kblk, vtblk, mask, state):
        out = []
        for mi in range(2):
            m_old, l_old = state[mi]
            s = jnp.dot(kblk, qtm[mi], preferred_element_type=F32)
            if mask is not None:
                s = jnp.where(mask, s, NEG_INF)
            m_new = jnp.maximum(m_old, jnp.max(s, axis=0, keepdims=True))
            alpha = jnp.exp(m_old - m_new)
            p = jnp.exp(s - m_new)
            l_new = alpha * l_old + jnp.sum(p, axis=0, keepdims=True)
            acc_ref[mi] = alpha * acc_ref[mi] + jnp.dot(vtblk, p.astype(BF16),
                                                        preferred_element_type=F32)
            out.append((m_new, l_new))
        return tuple(out)

    stat0 = (jnp.full((1, tq), NEG_INF, F32), jnp.zeros((1, tq), F32))
    key_m = lax.broadcasted_iota(I32, (META_ROWS, tq), 0)
    state = update(km_ref[...], vmt_ref[...], key_m < N_META, (stat0, stat0))

    def full_block(kb, state):
        o = pl.multiple_of(kb * tq, tq)
        return update(k_ref[0, pl.ds(o, tq), :], vt_ref[:, pl.ds(o, tq)], None, state)

    state = lax.fori_loop(0, qi, full_block, state)

    o = pl.multiple_of(qi * tq, tq)
    key_i = lax.broadcasted_iota(I32, (tq, tq), 0)
    qry_i = lax.broadcasted_iota(I32, (tq, tq), 1)
    state = update(k_ref[0, pl.ds(o, tq), :], vt_ref[:, pl.ds(o, tq)], key_i <= qry_i, state)

    lam = lam_ref[0]
    out = acc_ref[0] / state[0][1] - lam * (acc_ref[1] / state[1][1])
    ms = jnp.mean(out * out, axis=0, keepdims=True)
    out = out * lax.rsqrt(ms + EPS) * g_ref[...]
    o_ref[0] = jnp.transpose(out).astype(o_ref.dtype)


def _diff_attn(lam, qt, k, vt, k_meta, vt_meta, sub_g_b, tq):
    B, L, _ = k.shape
    nq = L // tq
    return pl.pallas_call(
        functools.partial(_attn_kernel, tq=tq),
        grid=(B, ATTN_HEADS, nq),
        in_specs=[
            pl.BlockSpec(memory_space=pltpu.SMEM),
            pl.BlockSpec((HEAD_W, tq), lambda b, h, i: (h, b * nq + i)),
            pl.BlockSpec((1, L, HEAD_W), lambda b, h, i: (b, 0, h)),
            pl.BlockSpec((HEAD_W, L), lambda b, h, i: (h, b)),
            pl.BlockSpec((META_ROWS, HEAD_W), lambda b, h, i: (0, h)),
            pl.BlockSpec((HEAD_W, META_ROWS), lambda b, h, i: (h, 0)),
            pl.BlockSpec((HEAD_W, tq), lambda b, h, i: (0, 0)),
        ],
        out_specs=pl.BlockSpec((1, tq, HEAD_W), lambda b, h, i: (b, i, h)),
        out_shape=jax.ShapeDtypeStruct((B, L, ATTN_WIDTH), BF16),
        scratch_shapes=[pltpu.VMEM((2, HEAD_W, tq), F32)],
        compiler_params=_cparams("arbitrary", "arbitrary", "arbitrary"),
        name="diff_attn",
    )(lam, qt, k, vt, k_meta, vt_meta, sub_g_b)


def _split_bf16(x):
    hi = x.astype(BF16)
    lo = (x - hi.astype(F32)).astype(BF16)
    return hi, lo


def _merge_kernel(x_ref, ys_ref, ya_ref, gmix_ref, wg_ref, wglu_ref, bglu_ref, wso_ref, wao_ref,
                  wo_ref, gffn_ref, wrh_ref, wrl_ref, br_ref, h_ref, xn_ref, lg_ref):
    x = x_ref[...]
    ms = jnp.mean(x * x, axis=-1, keepdims=True)
    n = (x * lax.rsqrt(ms + EPS) * gmix_ref[...]).astype(BF16)
    gates = jax.nn.sigmoid(jnp.dot(n, wg_ref[...], preferred_element_type=F32))

    ys = jax.nn.gelu(ys_ref[...])
    ys = ys * jax.nn.sigmoid(jnp.dot(ys.astype(BF16), wglu_ref[...], preferred_element_type=F32)
                             + bglu_ref[...])
    ys = jnp.dot(ys.astype(BF16), wso_ref[...], preferred_element_type=F32)
    ya = jnp.dot(ya_ref[...], wao_ref[...], preferred_element_type=F32)
    mixed = gates[:, :D_MODEL] * ys + gates[:, D_MODEL:] * ya
    h = x + jnp.dot(mixed.astype(BF16), wo_ref[...], preferred_element_type=F32)
    h_ref[...] = h

    ms2 = jnp.mean(h * h, axis=-1, keepdims=True)
    xn = h * lax.rsqrt(ms2 + EPS) * gffn_ref[...]
    tm = xn.shape[0]
    for s in range(SUBLANES):
        xn_ref[pl.ds(s, tm, stride=SUBLANES), :] = xn[:, LANES * s:LANES * (s + 1)]
    xh, xl = _split_bf16(xn)
    nt = (((1,), (1,)), ((), ()))
    lg = (lax.dot_general(wrh_ref[...], xh, nt, preferred_element_type=F32)
          + lax.dot_general(wrh_ref[...], xl, nt, preferred_element_type=F32)
          + lax.dot_general(wrl_ref[...], xh, nt, preferred_element_type=F32))
    lg_ref[...] = lg + br_ref[...]


def _merge(x2, ys, ya, gmix, wg, wglu, bglu, wso, wao, wo, gffn, wrh, wrl, br, tm):
    T = x2.shape[0]
    row = lambda i: (i, 0)
    fixed = lambda i: (0, 0)
    full = lambda a: pl.BlockSpec(a.shape, fixed)
    return pl.pallas_call(
        _merge_kernel,
        grid=(T // tm,),
        in_specs=[
            pl.BlockSpec((tm, D_MODEL), row),
            pl.BlockSpec((tm, SSM_WIDTH), row),
            pl.BlockSpec((tm, ATTN_WIDTH), row),
            full(gmix), full(wg), full(wglu), full(bglu), full(wso), full(wao), full(wo),
            full(gffn), full(wrh), full(wrl), full(br),
        ],
        out_specs=[
            pl.BlockSpec((tm, D_MODEL), row),
            pl.BlockSpec((tm * SUBLANES, LANES), row),
            pl.BlockSpec((N_EXPERTS, tm), lambda i: (0, i)),
        ],
        out_shape=[
            jax.ShapeDtypeStruct((T, D_MODEL), F32),
            jax.ShapeDtypeStruct((T * SUBLANES, LANES), F32),
            jax.ShapeDtypeStruct((N_EXPERTS, T), F32),
        ],
        compiler_params=_cparams("arbitrary"),
        name="merge",
    )(x2, ys, ya, gmix, wg, wglu, bglu, wso, wao, wo, gffn, wrh, wrl, br)


def _route_kernel(lg_ref, tri_ref, ltri_ref, gate_ref, dest_ref, cnt_ref, base_ref, pstart_ref, *, bm):
    ps = pl.program_id(0)
    i = pl.program_id(1)

    @pl.when(jnp.logical_and(ps == 0, i == 0))
    def _():
        base_ref[...] = jnp.zeros_like(base_ref)

    @pl.when(jnp.logical_and(ps == 1, i == 0))
    def _():
        total = base_ref[...]
        cnt_ref[...] = total
        nblk = jnp.floor((total + (bm - 1)) * (1.0 / bm))
        hi = jnp.floor(nblk * (1.0 / 16))
        lo = nblk - 16.0 * hi
        ex_hi = jnp.dot(ltri_ref[...], hi.astype(BF16), preferred_element_type=F32)
        ex_lo = jnp.dot(ltri_ref[...], lo.astype(BF16), preferred_element_type=F32)
        pstart_ref[...] = float(bm) * (16.0 * ex_hi + ex_lo)
        base_ref[...] = jnp.zeros_like(base_ref)

    work = lg_ref[...]
    row = lax.broadcasted_iota(I32, work.shape, 0).astype(F32)
    vals, onehots = [], []
    for _ in range(TOP_K):
        mx = jnp.max(work, axis=0, keepdims=True)
        ek = jnp.min(jnp.where(work == mx, row, float(N_EXPERTS)), axis=0, keepdims=True)
        oh = row == ek
        work = jnp.where(oh, -jnp.inf, work)
        vals.append(mx)
        onehots.append(oh)
    cnt = jnp.zeros(work.shape, F32)
    for oh in onehots:
        cnt = cnt + oh.astype(F32)

    @pl.when(ps == 1)
    def _():
        ex = [jnp.exp(v - vals[0]) for v in vals]
        den = ex[0] + ex[1] + ex[2] + ex[3]
        pre = (jnp.dot(cnt.astype(BF16), tri_ref[...], preferred_element_type=F32)
               + base_ref[:, 0:1] + pstart_ref[:, 0:1])
        for kk in range(TOP_K):
            gate_ref[kk:kk + 1, :] = ex[kk] / den
            dest_ref[kk:kk + 1, :] = jnp.sum(jnp.where(onehots[kk], pre, 0.0), axis=0,
                                             keepdims=True).astype(I32)

    base_ref[...] = base_ref[...] + jnp.sum(cnt, axis=1, keepdims=True)


def _route(logits_t, tt, bm):
    T = logits_t.shape[1]
    tri = jnp.asarray(np.triu(np.ones((tt, tt), np.float32), k=1), BF16)
    ltri = jnp.asarray(np.tril(np.ones((N_EXPERTS, N_EXPERTS), np.float32), k=-1), BF16)
    out_col = lambda ps, i: (0, i * ps)
    return pl.pallas_call(
        functools.partial(_route_kernel, bm=bm),
        grid=(2, T // tt),
        in_specs=[pl.BlockSpec((N_EXPERTS, tt), lambda ps, i: (0, i)),
                  pl.BlockSpec((tt, tt), lambda ps, i: (0, 0)),
                  pl.BlockSpec((N_EXPERTS, N_EXPERTS), lambda ps, i: (0, 0))],
        out_specs=[
            pl.BlockSpec((TOP_K, tt), out_col),
            pl.BlockSpec((TOP_K, tt), out_col),
            pl.BlockSpec((N_EXPERTS, LANES), lambda ps, i: (0, 0)),
        ],
        out_shape=[
            jax.ShapeDtypeStruct((TOP_K, T), F32),
            jax.ShapeDtypeStruct((TOP_K, T), I32),
            jax.ShapeDtypeStruct((N_EXPERTS, LANES), F32),
        ],
        scratch_shapes=[pltpu.VMEM((N_EXPERTS, LANES), F32), pltpu.VMEM((N_EXPERTS, LANES), F32)],
        compiler_params=_cparams("arbitrary", "arbitrary"),
        name="route",
    )(logits_t, tri, ltri)


def _invert_kernel(dest_ref, cnt_ref, pstart_ref, tab_ref, *, T, bm):
    A = T * TOP_K

    def pad_expert(e, c):
        first = pstart_ref[e] + cnt_ref[e]
        last = pstart_ref[e] + (cnt_ref[e] + (bm - 1)) // bm * bm

        def fill(r, c2):
            tab_ref[r] = A + r
            return c2
        lax.fori_loop(first, last, fill, 0)
        return last
    used = lax.fori_loop(0, N_EXPERTS, pad_expert, 0)

    def fill_tail(r, c):
        tab_ref[r] = A + r
        return c
    lax.fori_loop(used, tab_ref.shape[0], fill_tail, 0)

    def chunk(g, c):
        for q in range(DMA_UNROLL):
            a = g * DMA_UNROLL + q
            tab_ref[dest_ref[a]] = a
        return c
    lax.fori_loop(0, A // DMA_UNROLL, chunk, 0)


def _invert(dest_flat, cnt_i, pstart_i, T, n_rows, bm):
    smem = pl.BlockSpec(memory_space=pltpu.SMEM)
    return pl.pallas_call(
        functools.partial(_invert_kernel, T=T, bm=bm),
        in_specs=[smem, smem, smem],
        out_specs=smem,
        out_shape=jax.ShapeDtypeStruct((n_rows,), I32),
        name="invert",
    )(dest_flat, cnt_i, pstart_i)


def _experts_kernel(blke_ref, nused_ref, tab_ref, rend_ref, xn_hbm,
                    wg_ref, bg_ref, wu_ref, bu_ref, wd_ref, bd_ref, ya_hbm,
                    xb0, xb1, xb2, yb0, yb1, yb2, wgb, wub, wdb, gsem, ssem, *, bm, n_tok):
    i = pl.program_id(0)
    nu = nused_ref[0]
    xb = (xb0, xb1, xb2)
    yb = (yb0, yb1, yb2)

    def tile(ref, row):
        return ref.at[pl.ds(pl.multiple_of(row * SUBLANES, SUBLANES), SUBLANES), :]

    def row_in(v, r, s):
        t = v & (n_tok - 1) if n_tok & (n_tok - 1) == 0 else lax.rem(v, n_tok)
        return pltpu.make_async_copy(tile(xn_hbm, t), tile(xb[s], r), gsem.at[s])

    def row_out(v, r, s):
        return pltpu.make_async_copy(tile(yb[s], r), tile(ya_hbm, v), ssem.at[s])

    def valid_rows(blk):
        return jnp.minimum(bm, rend_ref[blke_ref[blk]] - blk * bm)

    def issue_in(blk, s):
        base = blk * bm
        for r in range(bm):
            row_in(tab_ref[base + r], r, s).start()

    def issue_out(blk, s):
        base = blk * bm
        nv = valid_rows(blk)
        for r in range(bm):
            @pl.when(r < nv)
            def _(r=r):
                row_out(tab_ref[base + r], r, s).start()

    def wait_in(s):
        pltpu.make_async_copy(xn_hbm.at[pl.ds(0, bm * SUBLANES), :], xb[s], gsem.at[s]).wait()

    def wait_out(blk, s):
        n8 = pl.multiple_of(valid_rows(blk) * SUBLANES, SUBLANES)
        pltpu.make_async_copy(yb[s].at[pl.ds(0, n8), :], ya_hbm.at[pl.ds(0, n8), :], ssem.at[s]).wait()

    @pl.when(i == 0)
    def _():
        issue_in(0, 0)

    def step(s, first):
        nxt, prv = (s + 1) % 3, (s + 2) % 3
        changed = jnp.logical_or(i == 0, blke_ref[i] != blke_ref[jnp.maximum(i - 1, 0)])

        @pl.when(changed)
        def _():
            wgb[...] = wg_ref[0].astype(BF16)
            wub[...] = wu_ref[0].astype(BF16)
            wdb[...] = wd_ref[0].astype(BF16)

        wait_in(s)
        if not first:
            @pl.when(i >= 3)
            def _():
                wait_out(i - 3, s)

        issue_in(i + 1, nxt)
        if not first:
            issue_out(i - 1, prv)
        hb = jnp.concatenate(
            [xb[s][pl.ds(c, bm, stride=SUBLANES), :] for c in range(SUBLANES)], axis=1).astype(BF16)
        g = jnp.minimum(jnp.dot(hb, wgb[...], preferred_element_type=F32) + bg_ref[0], SWIGLU_LIMIT)
        u = jnp.clip(jnp.dot(hb, wub[...], preferred_element_type=F32) + bu_ref[0],
                     -SWIGLU_LIMIT, SWIGLU_LIMIT)
        a = (u + 1.0) * (g * jax.nn.sigmoid(SWIGLU_ALPHA * g))
        y = jnp.dot(a.astype(BF16), wdb[...], preferred_element_type=F32) + bd_ref[0]
        for c in range(SUBLANES):
            yb[s][pl.ds(c, bm, stride=SUBLANES), :] = y[:, LANES * c:LANES * (c + 1)]

        @pl.when(i == nu - 1)
        def _():
            issue_out(i, s)
            wait_in(nxt)
            wait_out(i, s)
            if not first:
                wait_out(i - 1, prv)

                @pl.when(i >= 2)
                def _():
                    wait_out(i - 2, nxt)

    @pl.when(i == 0)
    def _():
        step(0, True)

    for s in range(3):
        @pl.when(jnp.logical_and(jnp.logical_and(i > 0, i < nu), i % 3 == s))
        def _(s=s):
            step(s, False)


def _experts(blk_e, n_used, tab, row_end, xn, w_gate, b_gate, w_up, b_up, w_down, b_down, bm):
    n_rows = tab.shape[0]
    n_blocks = n_rows // bm - 1
    n_tok = xn.shape[0] // SUBLANES
    E, D, Fd = w_gate.shape
    assert D == SUBLANES * LANES
    wmap = lambda i, be, nu, tb, re: (be[i], 0, 0)
    grid_spec = pltpu.PrefetchScalarGridSpec(
        num_scalar_prefetch=4,
        grid=(n_blocks,),
        in_specs=[
            pl.BlockSpec(memory_space=pl.ANY),
            pl.BlockSpec((1, D, Fd), wmap),
            pl.BlockSpec((1, 1, Fd), wmap),
            pl.BlockSpec((1, D, Fd), wmap),
            pl.BlockSpec((1, 1, Fd), wmap),
            pl.BlockSpec((1, Fd, D), wmap),
            pl.BlockSpec((1, 1, D), wmap),
        ],
        out_specs=pl.BlockSpec(memory_space=pl.ANY),
        scratch_shapes=[
            *[pltpu.VMEM((bm * SUBLANES, LANES), F32) for _ in range(6)],
            pltpu.VMEM((D, Fd), BF16),
            pltpu.VMEM((D, Fd), BF16),
            pltpu.VMEM((Fd, D), BF16),
            pltpu.SemaphoreType.DMA((3,)),
            pltpu.SemaphoreType.DMA((3,)),
        ],
    )
    return pl.pallas_call(
        functools.partial(_experts_kernel, bm=bm, n_tok=n_tok),
        grid_spec=grid_spec,
        out_shape=jax.ShapeDtypeStruct((n_tok * TOP_K * SUBLANES, LANES), F32),
        compiler_params=_cparams("arbitrary"),
        name="experts",
    )(blk_e, n_used, tab, row_end, xn, w_gate, b_gate, w_up, b_up, w_down, b_down)


def _combine_kernel(h_ref, y0_ref, y1_ref, y2_ref, y3_ref, g_ref, o_ref):
    tm = h_ref.shape[0]
    g = g_ref[...]
    for s in range(SUBLANES):
        lanes = slice(LANES * s, LANES * (s + 1))
        acc = h_ref[:, lanes]
        for kk, y_ref in enumerate((y0_ref, y1_ref, y2_ref, y3_ref)):
            acc = acc + g[:, kk:kk + 1] * y_ref[pl.ds(s, tm, stride=SUBLANES), :]
        o_ref[:, lanes] = acc


def _combine(h2, y_tiles, gate_t, tm):
    T = h2.shape[0]
    nt = T // tm
    plane = lambda kk: pl.BlockSpec((tm * SUBLANES, LANES), lambda i, kk=kk: (kk * nt + i, 0))
    return pl.pallas_call(
        _combine_kernel,
        grid=(nt,),
        in_specs=[pl.BlockSpec((tm, D_MODEL), lambda i: (i, 0)),
                  plane(0), plane(1), plane(2), plane(3),
                  pl.BlockSpec((tm, TOP_K), lambda i: (i, 0))],
        out_specs=pl.BlockSpec((tm, D_MODEL), lambda i: (i, 0)),
        out_shape=jax.ShapeDtypeStruct((T, D_MODEL), F32),
        compiler_params=_cparams("arbitrary"),
        name="combine",
    )(h2, y_tiles, y_tiles, y_tiles, y_tiles, gate_t)


def _layer(x, meta_tokens, norm_mix_g, w_in, ssm_a_re, ssm_a_im, ssm_log_dt, ssm_b_re, ssm_b_im,
           ssm_c_re, ssm_c_im, ssm_d, w_glu, b_glu, q_norm_g, k_norm_g, lambda_q1, lambda_k1,
           lambda_q2, lambda_k2, subln_g, w_ssm_out, w_attn_out, w_o, norm_ffn_g, w_router,
           b_router, w_gate, b_gate, w_up, b_up, w_down, b_down,
           *, tm=512, tc=256, tq=512, tt=512, bm=256):
    B, L, D = x.shape
    T = B * L
    G, P, C = SSM_GROUPS, SSM_STATE, SSM_GROUP
    o_g = SSM_WIDTH + 3 * ATTN_WIDTH

    w_uqkv = w_in[:, :o_g].astype(BF16)
    w_gates = w_in[:, o_g:].astype(BF16)
    gmix = norm_mix_g.reshape(1, D)
    scale = HEAD_DIM ** -0.5
    qg = (jnp.tile(q_norm_g, 2 * ATTN_HEADS) * scale).reshape(1, ATTN_WIDTH)
    kg = jnp.tile(k_norm_g, 2 * ATTN_HEADS).reshape(1, ATTN_WIDTH)
    gmat = jnp.asarray(np.kron(np.eye(2 * ATTN_HEADS), np.ones((HEAD_DIM, HEAD_DIM))) / HEAD_DIM, BF16)

    x2 = x.reshape(T, D)
    u_x, qt_x, k_x, vt_x = _in_proj(x2, gmix, w_uqkv, qg, kg, gmat, tm)
    meta_pad = jnp.zeros((META_ROWS, D), F32).at[:N_META].set(meta_tokens.astype(F32))
    u_m, _, k_m, vt_m = _in_proj(meta_pad, gmix, w_uqkv, qg, kg, gmat, META_ROWS)

    pw_re, pw_im, bbr_t, bbi_t = _s5_params(
        ssm_a_re.reshape(1, G * P), ssm_a_im.reshape(1, G * P),
        jnp.repeat(ssm_log_dt, P).reshape(1, G * P),
        jnp.transpose(ssm_b_re, (2, 0, 1)).reshape(C, G * P),
        jnp.transpose(ssm_b_im, (2, 0, 1)).reshape(C, G * P))
    bmat, cmat = _s5_matrices(bbr_t, bbi_t, ssm_c_re, ssm_c_im)
    u_meta_tile = jnp.zeros((tc, SSM_WIDTH), F32).at[tc - N_META:].set(u_m[:N_META])
    ys = _s5_scan(u_meta_tile, u_x.reshape(B, L, SSM_WIDTH), bmat, cmat, pw_re, pw_im,
                  ssm_d.reshape(1, SSM_WIDTH), tc)

    lam = (jnp.exp(jnp.sum(lambda_q1 * lambda_k1)) - jnp.exp(jnp.sum(lambda_q2 * lambda_k2))
           + LAMBDA_INIT).reshape(1).astype(F32)
    sub_g_b = jnp.broadcast_to((subln_g * (1.0 - LAMBDA_INIT)).reshape(HEAD_W, 1), (HEAD_W, tq))
    ya = _diff_attn(lam, qt_x, k_x.reshape(B, L, ATTN_WIDTH), vt_x, k_m, vt_m, sub_g_b, tq)

    wrh, wrl = _split_bf16(jnp.transpose(w_router))
    h2, xn, logits_t = _merge(
        x2, ys.reshape(T, SSM_WIDTH), ya.reshape(T, ATTN_WIDTH), gmix, w_gates,
        w_glu.astype(BF16), b_glu.reshape(1, SSM_WIDTH), w_ssm_out.astype(BF16),
        w_attn_out.astype(BF16), w_o.astype(BF16), norm_ffn_g.reshape(1, D),
        wrh, wrl, b_router.reshape(N_EXPERTS, 1), tm)

    n_blocks = T * TOP_K // bm + N_EXPERTS
    gate, dest, cnt = _route(logits_t, tt, bm)
    cnt_i = cnt[:, 0].astype(I32)
    nblk = (cnt_i + bm - 1) // bm
    blk_end = jnp.cumsum(nblk)
    pstart_i = (blk_end - nblk) * bm
    blk_e = jnp.minimum(jnp.sum((blk_end[None, :] <= jnp.arange(n_blocks, dtype=I32)[:, None]).astype(I32),
                                axis=1), N_EXPERTS - 1)
    n_used = blk_end[-1:].astype(I32)
    tab = _invert(dest.reshape(T * TOP_K), cnt_i, pstart_i, T, (n_blocks + 1) * bm, bm)
    y_assign = _experts(blk_e, n_used, tab, pstart_i + cnt_i, xn, w_gate, b_gate.reshape(N_EXPERTS, 1, -1),
                        w_up, b_up.reshape(N_EXPERTS, 1, -1), w_down, b_down.reshape(N_EXPERTS, 1, -1), bm)
    out = _combine(h2, y_assign, jnp.transpose(gate), tm)
    return out.reshape(B, L, D)


def kernel(x, meta_tokens, norm_mix_g, w_in, ssm_a_re, ssm_a_im, ssm_log_dt, ssm_b_re, ssm_b_im,
           ssm_c_re, ssm_c_im, ssm_d, w_glu, b_glu, q_norm_g, k_norm_g, lambda_q1, lambda_k1,
           lambda_q2, lambda_k2, subln_g, w_ssm_out, w_attn_out, w_o, norm_ffn_g, w_router,
           b_router, w_gate, b_gate, w_up, b_up, w_down, b_down):
    p = [a[0] for a in (norm_mix_g, w_in, ssm_a_re, ssm_a_im, ssm_log_dt, ssm_b_re, ssm_b_im,
                        ssm_c_re, ssm_c_im, ssm_d, w_glu, b_glu, q_norm_g, k_norm_g, lambda_q1,
                        lambda_k1, lambda_q2, lambda_k2, subln_g, w_ssm_out, w_attn_out, w_o,
                        norm_ffn_g, w_router, b_router, w_gate, b_gate, w_up, b_up, w_down, b_down)]
    return _layer(x, meta_tokens, *p)
```

```python
import functools
import math

import numpy as np
import jax
import jax.numpy as jnp
from jax import lax
from jax.experimental import pallas as pl
from jax.experimental.pallas import tpu as pltpu

F32 = jnp.float32
BF16 = jnp.bfloat16
I32 = jnp.int32

D_MODEL = 1024
N_META = 16
SSM_WIDTH = 512
SSM_GROUP = 16
SSM_GROUPS = 32
SSM_STATE = 64
A_RE_MAX = -1e-4
ATTN_HEADS = 4
HEAD_DIM = 64
ATTN_WIDTH = 512
NEG_INF = -1e30
N_EXPERTS = 32
TOP_K = 4
SWIGLU_LIMIT = 7.0
SWIGLU_ALPHA = 1.702
EPS = 1e-6
LAMBDA_INIT = 0.8 - 0.6 * math.exp(-0.3 * 0)

LANES = 128
SUBLANES = 8
META_ROWS = 128
VMEM_LIMIT = 56 * 1024 * 1024
DMA_UNROLL = 8

N_STATE = SSM_GROUPS * SSM_STATE
STATE_TILES = N_STATE // LANES
HEAD_W = 2 * HEAD_DIM


def _cparams(*sem):
    return pltpu.CompilerParams(dimension_semantics=sem, vmem_limit_bytes=VMEM_LIMIT)


def _in_proj_kernel(x_ref, g_ref, w_ref, qg_ref, kg_ref, gm_ref, u_ref, qt_ref, k_ref, vt_ref):
    x = x_ref[...]
    ms = jnp.mean(x * x, axis=-1, keepdims=True)
    n = (x * lax.rsqrt(ms + EPS) * g_ref[...]).astype(BF16)
    z = jnp.dot(n, w_ref[...], preferred_element_type=F32)
    u_ref[...] = z[:, :SSM_WIDTH]
    gm = gm_ref[...]

    def head_norm(t, g):
        ms_g = jnp.dot((t * t).astype(BF16), gm, preferred_element_type=F32)
        return t * lax.rsqrt(ms_g + EPS) * g

    q = z[:, SSM_WIDTH:SSM_WIDTH + ATTN_WIDTH]
    k = z[:, SSM_WIDTH + ATTN_WIDTH:SSM_WIDTH + 2 * ATTN_WIDTH]
    qt_ref[...] = jnp.transpose(head_norm(q, qg_ref[...])).astype(BF16)
    k_ref[...] = head_norm(k, kg_ref[...]).astype(BF16)
    vt_ref[...] = jnp.transpose(z[:, SSM_WIDTH + 2 * ATTN_WIDTH:]).astype(BF16)


def _in_proj(x2, g_mix, w_uqkv, qg, kg, gmat, tm):
    n_rows = x2.shape[0]
    wdt = w_uqkv.shape[1]
    row = lambda i: (i, 0)
    col = lambda i: (0, i)
    fixed = lambda i: (0, 0)
    return pl.pallas_call(
        _in_proj_kernel,
        grid=(n_rows // tm,),
        in_specs=[
            pl.BlockSpec((tm, D_MODEL), row),
            pl.BlockSpec((1, D_MODEL), fixed),
            pl.BlockSpec((D_MODEL, wdt), fixed),
            pl.BlockSpec((1, ATTN_WIDTH), fixed),
            pl.BlockSpec((1, ATTN_WIDTH), fixed),
            pl.BlockSpec((ATTN_WIDTH, ATTN_WIDTH), fixed),
        ],
        out_specs=[
            pl.BlockSpec((tm, SSM_WIDTH), row),
            pl.BlockSpec((ATTN_WIDTH, tm), col),
            pl.BlockSpec((tm, ATTN_WIDTH), row),
            pl.BlockSpec((ATTN_WIDTH, tm), col),
        ],
        out_shape=[
            jax.ShapeDtypeStruct((n_rows, SSM_WIDTH), F32),
            jax.ShapeDtypeStruct((ATTN_WIDTH, n_rows), BF16),
            jax.ShapeDtypeStruct((n_rows, ATTN_WIDTH), BF16),
            jax.ShapeDtypeStruct((ATTN_WIDTH, n_rows), BF16),
        ],
        compiler_params=_cparams("arbitrary"),
        name="in_proj",
    )(x2, g_mix, w_uqkv, qg, kg, gmat)


def _s5_params_kernel(are_ref, aim_ref, ldt_ref, bre_ref, bim_ref,
                      pwr_ref, pwi_ref, bbr_ref, bbi_ref):
    lr = jnp.minimum(are_ref[...], A_RE_MAX)
    li = aim_ref[...]
    dt = jnp.exp(ldt_ref[...])
    mag = jnp.exp(lr * dt)
    ar = mag * jnp.cos(li * dt)
    ai = mag * jnp.sin(li * dt)
    den = lr * lr + li * li
    fr = ((ar - 1.0) * lr + ai * li) / den
    fi = (ai * lr - (ar - 1.0) * li) / den
    br = bre_ref[...]
    bi = bim_ref[...]
    bbr_ref[...] = fr * br - fi * bi
    bbi_ref[...] = fr * bi + fi * br
    pr, pi = ar, ai
    pwr_ref[0:1, :] = pr
    pwi_ref[0:1, :] = pi
    for t in range(1, SUBLANES):
        pr, pi = pr * ar - pi * ai, pr * ai + pi * ar
        pwr_ref[t:t + 1, :] = pr
        pwi_ref[t:t + 1, :] = pi


def _s5_params(a_re, a_im, log_dt_b, b_re_t, b_im_t):
    return pl.pallas_call(
        _s5_params_kernel,
        out_shape=[
            jax.ShapeDtypeStruct((SUBLANES, N_STATE), F32),
            jax.ShapeDtypeStruct((SUBLANES, N_STATE), F32),
            jax.ShapeDtypeStruct((SSM_GROUP, N_STATE), F32),
            jax.ShapeDtypeStruct((SSM_GROUP, N_STATE), F32),
        ],
        name="s5_params",
    )(a_re, a_im, log_dt_b, b_re_t, b_im_t)


def _s5_matrices(bbr_t, bbi_t, c_re, c_im):
    G, P, C = SSM_GROUPS, SSM_STATE, SSM_GROUP
    bb = jnp.stack([bbr_t, bbi_t]).reshape(2, C, G, P)
    j = np.arange(STATE_TILES)
    g_of = (2 * j[:, None] + np.arange(2)[None, :])
    v = bb[:, :, g_of, :]
    v = jnp.transpose(v, (2, 1, 0, 3, 4))
    gl = np.arange(8)
    mask_b = (8 * (j[:, None, None] // 4) + gl[None, :, None] == g_of[:, None, :])
    bmat = (jnp.asarray(mask_b, F32)[:, :, None, None, :, None] * v[:, None])
    bmat = bmat.reshape(STATE_TILES, 8 * C, 2 * 2 * P).astype(BF16)

    cc = jnp.stack([c_re, -c_im])
    w = cc[:, g_of, :, :]
    w = jnp.transpose(w, (1, 0, 2, 4, 3))
    gl16 = np.arange(16)
    mask_c = (16 * (j[:, None, None] // 8) + gl16[None, None, :] == g_of[:, :, None])
    cmat = (w[:, :, :, :, None, :] * jnp.asarray(mask_c, F32)[:, None, :, None, :, None])
    cmat = cmat.reshape(STATE_TILES, 2 * 2 * P, 16 * C).astype(BF16)
    return bmat, cmat


def _s5_scan_kernel(um_ref, ux_ref, bmat_ref, cmat_ref, pwr_ref, pwi_ref, d_ref,
                    y_ref, s_ref, cr_ref, ci_ref, *, tc, tiles_per_loop):
    i = pl.program_id(1)

    @pl.when(i == 0)
    def _():
        cr_ref[...] = jnp.zeros_like(cr_ref)
        ci_ref[...] = jnp.zeros_like(ci_ref)

    u = jnp.where(i == 0, um_ref[...], ux_ref[0])
    ub = u.astype(BF16)

    for j in range(STATE_TILES):
        c0 = 2 * LANES * j
        k0 = LANES * (j // 4)
        s_ref[:, c0:c0 + 2 * LANES] = jnp.dot(ub[:, k0:k0 + LANES], bmat_ref[j],
                                              preferred_element_type=F32)

    row = lax.broadcasted_iota(I32, (SUBLANES, LANES), 0)
    for j0 in range(0, STATE_TILES, tiles_per_loop):
        tiles = list(range(j0, j0 + tiles_per_loop))
        consts = []
        for j in tiles:
            lanes = slice(LANES * j, LANES * (j + 1))
            pr = pwr_ref[:, lanes]
            pi = pwi_ref[:, lanes]
            coef = []
            for d in (1, 2, 4):
                m = row >= d
                coef.append((jnp.where(m, jnp.broadcast_to(pr[d - 1:d, :], (SUBLANES, LANES)), 0.0),
                             jnp.where(m, jnp.broadcast_to(pi[d - 1:d, :], (SUBLANES, LANES)), 0.0)))
            consts.append((pr, pi, coef))

        def group(r, carry, tiles=tiles, consts=consts):
            o = pl.multiple_of(r * SUBLANES, SUBLANES)
            out = []
            for j, (pr, pi, coef), (cre, cim) in zip(tiles, consts, carry):
                c0 = 2 * LANES * j
                xr = s_ref[pl.ds(o, SUBLANES), c0:c0 + LANES]
                xi = s_ref[pl.ds(o, SUBLANES), c0 + LANES:c0 + 2 * LANES]
                for d, (kr, ki) in zip((1, 2, 4), coef):
                    sr = pltpu.roll(xr, d, 0)
                    si = pltpu.roll(xi, d, 0)
                    xr, xi = xr + (kr * sr - ki * si), xi + (kr * si + ki * sr)
                xr = xr + (pr * cre - pi * cim)
                xi = xi + (pr * cim + pi * cre)
                s_ref[pl.ds(o, SUBLANES), c0:c0 + LANES] = xr
                s_ref[pl.ds(o, SUBLANES), c0 + LANES:c0 + 2 * LANES] = xi
                out.append((jnp.broadcast_to(xr[SUBLANES - 1:SUBLANES, :], (SUBLANES, LANES)),
                            jnp.broadcast_to(xi[SUBLANES - 1:SUBLANES, :], (SUBLANES, LANES))))
            return tuple(out)

        init = tuple((cr_ref[:, LANES * j:LANES * (j + 1)], ci_ref[:, LANES * j:LANES * (j + 1)])
                     for j in tiles)
        fin = lax.fori_loop(0, tc // SUBLANES, group, init)
        for j, (cre, cim) in zip(tiles, fin):
            cr_ref[:, LANES * j:LANES * (j + 1)] = cre
            ci_ref[:, LANES * j:LANES * (j + 1)] = cim

    half = STATE_TILES // 2
    for n in range(2):
        acc = jnp.zeros((tc, 2 * LANES), F32)
        for j in range(half * n, half * (n + 1)):
            c0 = 2 * LANES * j
            acc += jnp.dot(s_ref[:, c0:c0 + 2 * LANES].astype(BF16), cmat_ref[j],
                           preferred_element_type=F32)
        cols = slice(2 * LANES * n, 2 * LANES * (n + 1))
        y_ref[0, :, cols] = acc + d_ref[:, cols] * u[:, cols]


def _s5_scan(u_meta_tile, u_x, bmat, cmat, pw_re, pw_im, d_row, tc, tiles_per_loop=4):
    B, L, _ = u_x.shape
    nt = L // tc
    xmap = lambda b, i: (b, jnp.maximum(i - 1, 0), 0)
    fixed2 = lambda b, i: (0, 0)
    fixed3 = lambda b, i: (0, 0, 0)
    return pl.pallas_call(
        functools.partial(_s5_scan_kernel, tc=tc, tiles_per_loop=tiles_per_loop),
        grid=(B, nt + 1),
        in_specs=[
            pl.BlockSpec((tc, SSM_WIDTH), fixed2),
            pl.BlockSpec((1, tc, SSM_WIDTH), xmap),
            pl.BlockSpec(bmat.shape, fixed3),
            pl.BlockSpec(cmat.shape, fixed3),
            pl.BlockSpec((SUBLANES, N_STATE), fixed2),
            pl.BlockSpec((SUBLANES, N_STATE), fixed2),
            pl.BlockSpec((1, SSM_WIDTH), fixed2),
        ],
        out_specs=pl.BlockSpec((1, tc, SSM_WIDTH), xmap),
        out_shape=jax.ShapeDtypeStruct((B, L, SSM_WIDTH), F32),
        scratch_shapes=[
            pltpu.VMEM((tc, 2 * N_STATE), F32),
            pltpu.VMEM((SUBLANES, N_STATE), F32),
            pltpu.VMEM((SUBLANES, N_STATE), F32),
        ],
        compiler_params=_cparams("arbitrary", "arbitrary"),
        name="s5_scan",
    )(u_meta_tile, u_x, bmat, cmat, pw_re, pw_im, d_row)


def _attn_kernel(lam_ref, qt_ref, k_ref, vt_ref, km_ref, vmt_ref, g_ref, o_ref,
                 acc_ref, st_ref, sa_ref, sb_ref, bxa_ref, bxb_ref, *, tq):
    qi = pl.program_id(2)
    qt = qt_ref[...]
    feat = lax.broadcasted_iota(I32, qt.shape, 0)
    zero = jnp.zeros_like(qt)
    qtm = (jnp.where(feat < HEAD_DIM, qt, zero), jnp.where(feat >= HEAD_DIM, qt, zero))
    acc_ref[...] = jnp.zeros_like(acc_ref)
    for mi in range(2):
        st_ref[2 * mi:2 * mi + 1, :] = jnp.full((1, tq), NEG_INF, F32)
        st_ref[2 * mi + 1:2 * mi + 2, :] = jnp.zeros((1, tq), F32)
    bufs = ((sa_ref, bxa_ref), (sb_ref, bxb_ref))

    def score(kblk, buf, mask):
        s_ref, bx_ref = bufs[buf]
        rows = kblk.shape[0]
        for mi in range(2):
            s = jnp.dot(kblk, qtm[mi], preferred_element_type=F32)
            if mask is not None:
                s = jnp.where(mask, s, NEG_INF)
            s_ref[mi, :rows, :] = s
            bx_ref[mi:mi + 1, :] = jnp.max(s, axis=0, keepdims=True)

    def update(vtblk, buf):
        s_ref, bx_ref = bufs[buf]
        rows = vtblk.shape[1]
        for mi in range(2):
            m_old = st_ref[2 * mi:2 * mi + 1, :]
            m_new = jnp.maximum(m_old, bx_ref[mi:mi + 1, :])
            alpha = jnp.exp(m_old - m_new)
            p = jnp.exp(s_ref[mi, :rows, :] - m_new)
            st_ref[2 * mi + 1:2 * mi + 2, :] = (alpha * st_ref[2 * mi + 1:2 * mi + 2, :]
                                                + jnp.sum(p, axis=0, keepdims=True))
            st_ref[2 * mi:2 * mi + 1, :] = m_new
            acc_ref[mi] = alpha * acc_ref[mi] + jnp.dot(vtblk, p.astype(BF16),
                                                        preferred_element_type=F32)

    def k_block(kb):
        return k_ref[0, pl.ds(pl.multiple_of(kb * tq, tq), tq), :]

    def vt_block(kb):
        return vt_ref[:, pl.ds(pl.multiple_of(kb * tq, tq), tq)]

    key_m = lax.broadcasted_iota(I32, (META_ROWS, tq), 0)
    score(km_ref[...], 0, key_m < N_META)
    update(vmt_ref[...], 0)

    def causal():
        return (lax.broadcasted_iota(I32, (tq, tq), 0) <= lax.broadcasted_iota(I32, (tq, tq), 1))

    @pl.when(qi == 0)
    def _():
        score(k_block(0), 0, causal())
        update(vt_block(0), 0)

    @pl.when(qi > 0)
    def _():
        score(k_block(0), 0, None)
        n_pair = (qi - 1) // 2

        def pair(t, c):
            score(k_block(2 * t + 1), 1, None)
            update(vt_block(2 * t), 0)
            score(k_block(2 * t + 2), 0, None)
            update(vt_block(2 * t + 1), 1)
            return c
        lax.fori_loop(0, n_pair, pair, 0)
        done = 2 * n_pair

        @pl.when(qi - done == 1)
        def _():
            score(k_block(qi), 1, causal())
            update(vt_block(done), 0)
            update(vt_block(qi), 1)

        @pl.when(qi - done == 2)
        def _():
            score(k_block(done + 1), 1, None)
            update(vt_block(done), 0)
            score(k_block(qi), 0, causal())
            update(vt_block(done + 1), 1)
            update(vt_block(qi), 0)

    lam = lam_ref[0]
    out = acc_ref[0] / st_ref[1:2, :] - lam * (acc_ref[1] / st_ref[3:4, :])
    ms = jnp.mean(out * out, axis=0, keepdims=True)
    out = out * lax.rsqrt(ms + EPS) * g_ref[...]
    o_ref[0] = jnp.transpose(out).astype(o_ref.dtype)


def _diff_attn(lam, qt, k, vt, k_meta, vt_meta, sub_g_b, tq):
    B, L, _ = k.shape
    nq = L // tq
    return pl.pallas_call(
        functools.partial(_attn_kernel, tq=tq),
        grid=(B, ATTN_HEADS, nq),
        in_specs=[
            pl.BlockSpec(memory_space=pltpu.SMEM),
            pl.BlockSpec((HEAD_W, tq), lambda b, h, i: (h, b * nq + i)),
            pl.BlockSpec((1, L, HEAD_W), lambda b, h, i: (b, 0, h)),
            pl.BlockSpec((HEAD_W, L), lambda b, h, i: (h, b)),
            pl.BlockSpec((META_ROWS, HEAD_W), lambda b, h, i: (0, h)),
            pl.BlockSpec((HEAD_W, META_ROWS), lambda b, h, i: (h, 0)),
            pl.BlockSpec((HEAD_W, tq), lambda b, h, i: (0, 0)),
        ],
        out_specs=pl.BlockSpec((1, tq, HEAD_W), lambda b, h, i: (b, i, h)),
        out_shape=jax.ShapeDtypeStruct((B, L, ATTN_WIDTH), BF16),
        scratch_shapes=[
            pltpu.VMEM((2, HEAD_W, tq), F32),
            pltpu.VMEM((SUBLANES, tq), F32),
            pltpu.VMEM((2, tq, tq), F32),
            pltpu.VMEM((2, tq, tq), F32),
            pltpu.VMEM((SUBLANES, tq), F32),
            pltpu.VMEM((SUBLANES, tq), F32),
        ],
        compiler_params=_cparams("arbitrary", "arbitrary", "arbitrary"),
        name="diff_attn",
    )(lam, qt, k, vt, k_meta, vt_meta, sub_g_b)


def _split_bf16(x):
    hi = x.astype(BF16)
    lo = (x - hi.astype(F32)).astype(BF16)
    return hi, lo


def _merge_kernel(x_ref, ys_ref, ya_ref, gmix_ref, wg_ref, wglu_ref, bglu_ref, wso_ref, wao_ref,
                  wo_ref, gffn_ref, wrh_ref, wrl_ref, br_ref, h_ref, xn_ref, lg_ref):
    x = x_ref[...]
    ms = jnp.mean(x * x, axis=-1, keepdims=True)
    n = (x * lax.rsqrt(ms + EPS) * gmix_ref[...]).astype(BF16)
    gates = jax.nn.sigmoid(jnp.dot(n, wg_ref[...], preferred_element_type=F32))

    ys = jax.nn.gelu(ys_ref[...])
    ys = ys * jax.nn.sigmoid(jnp.dot(ys.astype(BF16), wglu_ref[...], preferred_element_type=F32)
                             + bglu_ref[...])
    ys = jnp.dot(ys.astype(BF16), wso_ref[...], preferred_element_type=F32)
    ya = jnp.dot(ya_ref[...], wao_ref[...], preferred_element_type=F32)
    mixed = gates[:, :D_MODEL] * ys + gates[:, D_MODEL:] * ya
    h = x + jnp.dot(mixed.astype(BF16), wo_ref[...], preferred_element_type=F32)
    h_ref[...] = h

    ms2 = jnp.mean(h * h, axis=-1, keepdims=True)
    xn = h * lax.rsqrt(ms2 + EPS) * gffn_ref[...]
    tm = xn.shape[0]
    for s in range(SUBLANES):
        xn_ref[pl.ds(s, tm, stride=SUBLANES), :] = xn[:, LANES * s:LANES * (s + 1)]
    xh, xl = _split_bf16(xn)
    nt = (((1,), (1,)), ((), ()))
    lg = (lax.dot_general(wrh_ref[...], xh, nt, preferred_element_type=F32)
          + lax.dot_general(wrh_ref[...], xl, nt, preferred_element_type=F32)
          + lax.dot_general(wrl_ref[...], xh, nt, preferred_element_type=F32))
    lg_ref[...] = lg + br_ref[...]


def _merge(x2, ys, ya, gmix, wg, wglu, bglu, wso, wao, wo, gffn, wrh, wrl, br, tm):
    T = x2.shape[0]
    row = lambda i: (i, 0)
    fixed = lambda i: (0, 0)
    full = lambda a: pl.BlockSpec(a.shape, fixed)
    return pl.pallas_call(
        _merge_kernel,
        grid=(T // tm,),
        in_specs=[
            pl.BlockSpec((tm, D_MODEL), row),
            pl.BlockSpec((tm, SSM_WIDTH), row),
            pl.BlockSpec((tm, ATTN_WIDTH), row),
            full(gmix), full(wg), full(wglu), full(bglu), full(wso), full(wao), full(wo),
            full(gffn), full(wrh), full(wrl), full(br),
        ],
        out_specs=[
            pl.BlockSpec((tm, D_MODEL), row),
            pl.BlockSpec((tm * SUBLANES, LANES), row),
            pl.BlockSpec((N_EXPERTS, tm), lambda i: (0, i)),
        ],
        out_shape=[
            jax.ShapeDtypeStruct((T, D_MODEL), F32),
            jax.ShapeDtypeStruct((T * SUBLANES, LANES), F32),
            jax.ShapeDtypeStruct((N_EXPERTS, T), F32),
        ],
        compiler_params=_cparams("arbitrary"),
        name="merge",
    )(x2, ys, ya, gmix, wg, wglu, bglu, wso, wao, wo, gffn, wrh, wrl, br)


def _route_kernel(lg_ref, tri_ref, ltri_ref, gate_ref, dest_ref, cnt_ref, base_ref, pstart_ref, *, bm):
    ps = pl.program_id(0)
    i = pl.program_id(1)

    @pl.when(jnp.logical_and(ps == 0, i == 0))
    def _():
        base_ref[...] = jnp.zeros_like(base_ref)

    @pl.when(jnp.logical_and(ps == 1, i == 0))
    def _():
        total = base_ref[...]
        cnt_ref[...] = total
        nblk = jnp.floor((total + (bm - 1)) * (1.0 / bm))
        hi = jnp.floor(nblk * (1.0 / 16))
        lo = nblk - 16.0 * hi
        ex_hi = jnp.dot(ltri_ref[...], hi.astype(BF16), preferred_element_type=F32)
        ex_lo = jnp.dot(ltri_ref[...], lo.astype(BF16), preferred_element_type=F32)
        pstart_ref[...] = float(bm) * (16.0 * ex_hi + ex_lo)
        base_ref[...] = jnp.zeros_like(base_ref)

    work = lg_ref[...]
    row = lax.broadcasted_iota(I32, work.shape, 0).astype(F32)
    vals, onehots = [], []
    for _ in range(TOP_K):
        mx = jnp.max(work, axis=0, keepdims=True)
        ek = jnp.min(jnp.where(work == mx, row, float(N_EXPERTS)), axis=0, keepdims=True)
        oh = row == ek
        work = jnp.where(oh, -jnp.inf, work)
        vals.append(mx)
        onehots.append(oh)
    cnt = jnp.zeros(work.shape, F32)
    for oh in onehots:
        cnt = cnt + oh.astype(F32)

    @pl.when(ps == 1)
    def _():
        ex = [jnp.exp(v - vals[0]) for v in vals]
        den = ex[0] + ex[1] + ex[2] + ex[3]
        pre = (jnp.dot(cnt.astype(BF16), tri_ref[...], preferred_element_type=F32)
               + base_ref[:, 0:1] + pstart_ref[:, 0:1])
        for kk in range(TOP_K):
            gate_ref[kk:kk + 1, :] = ex[kk] / den
            dest_ref[kk:kk + 1, :] = jnp.sum(jnp.where(onehots[kk], pre, 0.0), axis=0,
                                             keepdims=True).astype(I32)

    base_ref[...] = base_ref[...] + jnp.sum(cnt, axis=1, keepdims=True)


def _route(logits_t, tt, bm):
    T = logits_t.shape[1]
    tri = jnp.asarray(np.triu(np.ones((tt, tt), np.float32), k=1), BF16)
    ltri = jnp.asarray(np.tril(np.ones((N_EXPERTS, N_EXPERTS), np.float32), k=-1), BF16)
    out_col = lambda ps, i: (0, i * ps)
    return pl.pallas_call(
        functools.partial(_route_kernel, bm=bm),
        grid=(2, T // tt),
        in_specs=[pl.BlockSpec((N_EXPERTS, tt), lambda ps, i: (0, i)),
                  pl.BlockSpec((tt, tt), lambda ps, i: (0, 0)),
                  pl.BlockSpec((N_EXPERTS, N_EXPERTS), lambda ps, i: (0, 0))],
        out_specs=[
            pl.BlockSpec((TOP_K, tt), out_col),
            pl.BlockSpec((TOP_K, tt), out_col),
            pl.BlockSpec((N_EXPERTS, LANES), lambda ps, i: (0, 0)),
        ],
        out_shape=[
            jax.ShapeDtypeStruct((TOP_K, T), F32),
            jax.ShapeDtypeStruct((TOP_K, T), I32),
            jax.ShapeDtypeStruct((N_EXPERTS, LANES), F32),
        ],
        scratch_shapes=[pltpu.VMEM((N_EXPERTS, LANES), F32), pltpu.VMEM((N_EXPERTS, LANES), F32)],
        compiler_params=_cparams("arbitrary", "arbitrary"),
        name="route",
    )(logits_t, tri, ltri)


def _invert_kernel(dest_ref, cnt_ref, pstart_ref, tab_ref, *, T, bm):
    A = T * TOP_K

    def pad_expert(e, c):
        first = pstart_ref[e] + cnt_ref[e]
        last = pstart_ref[e] + (cnt_ref[e] + (bm - 1)) // bm * bm

        def fill(r, c2):
            tab_ref[r] = A + r
            return c2
        lax.fori_loop(first, last, fill, 0)
        return last
    used = lax.fori_loop(0, N_EXPERTS, pad_expert, 0)

    def fill_tail(r, c):
        tab_ref[r] = A + r
        return c
    lax.fori_loop(used, tab_ref.shape[0], fill_tail, 0)

    def chunk(g, c):
        for q in range(DMA_UNROLL):
            a = g * DMA_UNROLL + q
            tab_ref[dest_ref[a]] = a
        return c
    lax.fori_loop(0, A // DMA_UNROLL, chunk, 0)


def _invert(dest_flat, cnt_i, pstart_i, T, n_rows, bm):
    smem = pl.BlockSpec(memory_space=pltpu.SMEM)
    return pl.pallas_call(
        functools.partial(_invert_kernel, T=T, bm=bm),
        in_specs=[smem, smem, smem],
        out_specs=smem,
        out_shape=jax.ShapeDtypeStruct((n_rows,), I32),
        name="invert",
    )(dest_flat, cnt_i, pstart_i)


def _experts_kernel(blke_ref, nused_ref, tab_ref, rend_ref, xn_hbm,
                    wg_ref, bg_ref, wu_ref, bu_ref, wd_ref, bd_ref, ya_hbm,
                    xb0, xb1, xb2, yb0, yb1, yb2, wgb, wub, wdb, gsem, ssem, *, bm, n_tok):
    i = pl.program_id(0)
    nu = nused_ref[0]
    xb = (xb0, xb1, xb2)
    yb = (yb0, yb1, yb2)

    def tile(ref, row):
        return ref.at[pl.ds(pl.multiple_of(row * SUBLANES, SUBLANES), SUBLANES), :]

    def row_in(v, r, s):
        t = v & (n_tok - 1) if n_tok & (n_tok - 1) == 0 else lax.rem(v, n_tok)
        return pltpu.make_async_copy(tile(xn_hbm, t), tile(xb[s], r), gsem.at[s])

    def row_out(v, r, s):
        return pltpu.make_async_copy(tile(yb[s], r), tile(ya_hbm, v), ssem.at[s])

    def valid_rows(blk):
        return jnp.minimum(bm, rend_ref[blke_ref[blk]] - blk * bm)

    def issue_in(blk, s):
        base = blk * bm
        for r in range(bm):
            row_in(tab_ref[base + r], r, s).start(priority=r % 2)

    def issue_out(blk, s):
        base = blk * bm
        nv = valid_rows(blk)
        for r in range(bm):
            @pl.when(r < nv)
            def _(r=r):
                row_out(tab_ref[base + r], r, s).start(priority=r % 2)

    def wait_in(s):
        pltpu.make_async_copy(xn_hbm.at[pl.ds(0, bm * SUBLANES), :], xb[s], gsem.at[s]).wait()

    def wait_out(blk, s):
        n8 = pl.multiple_of(valid_rows(blk) * SUBLANES, SUBLANES)
        pltpu.make_async_copy(yb[s].at[pl.ds(0, n8), :], ya_hbm.at[pl.ds(0, n8), :], ssem.at[s]).wait()

    @pl.when(i == 0)
    def _():
        issue_in(0, 0)

    def step(s, first):
        nxt, prv = (s + 1) % 3, (s + 2) % 3
        changed = jnp.logical_or(i == 0, blke_ref[i] != blke_ref[jnp.maximum(i - 1, 0)])

        @pl.when(changed)
        def _():
            wgb[...] = wg_ref[0].astype(BF16)
            wub[...] = wu_ref[0].astype(BF16)
            wdb[...] = wd_ref[0].astype(BF16)

        wait_in(s)
        if not first:
            @pl.when(i >= 3)
            def _():
                wait_out(i - 3, s)

        issue_in(i + 1, nxt)
        if not first:
            issue_out(i - 1, prv)
        hb = jnp.concatenate(
            [xb[s][pl.ds(c, bm, stride=SUBLANES), :] for c in range(SUBLANES)], axis=1).astype(BF16)
        g = jnp.minimum(jnp.dot(hb, wgb[...], preferred_element_type=F32) + bg_ref[0], SWIGLU_LIMIT)
        u = jnp.clip(jnp.dot(hb, wub[...], preferred_element_type=F32) + bu_ref[0],
                     -SWIGLU_LIMIT, SWIGLU_LIMIT)
        a = (u + 1.0) * (g * jax.nn.sigmoid(SWIGLU_ALPHA * g))
        y = jnp.dot(a.astype(BF16), wdb[...], preferred_element_type=F32) + bd_ref[0]
        for c in range(SUBLANES):
            yb[s][pl.ds(c, bm, stride=SUBLANES), :] = y[:, LANES * c:LANES * (c + 1)]

        @pl.when(i == nu - 1)
        def _():
            issue_out(i, s)
            wait_in(nxt)
            wait_out(i, s)
            if not first:
                wait_out(i - 1, prv)

                @pl.when(i >= 2)
                def _():
                    wait_out(i - 2, nxt)

    @pl.when(i == 0)
    def _():
        step(0, True)

    for s in range(3):
        @pl.when(jnp.logical_and(jnp.logical_and(i > 0, i < nu), i % 3 == s))
        def _(s=s):
            step(s, False)


def _experts(blk_e, n_used, tab, row_end, xn, w_gate, b_gate, w_up, b_up, w_down, b_down, bm):
    n_rows = tab.shape[0]
    n_blocks = n_rows // bm - 1
    n_tok = xn.shape[0] // SUBLANES
    E, D, Fd = w_gate.shape
    assert D == SUBLANES * LANES
    wmap = lambda i, be, nu, tb, re: (be[i], 0, 0)
    grid_spec = pltpu.PrefetchScalarGridSpec(
        num_scalar_prefetch=4,
        grid=(n_blocks,),
        in_specs=[
            pl.BlockSpec(memory_space=pl.ANY),
            pl.BlockSpec((1, D, Fd), wmap),
            pl.BlockSpec((1, 1, Fd), wmap),
            pl.BlockSpec((1, D, Fd), wmap),
            pl.BlockSpec((1, 1, Fd), wmap),
            pl.BlockSpec((1, Fd, D), wmap),
            pl.BlockSpec((1, 1, D), wmap),
        ],
        out_specs=pl.BlockSpec(memory_space=pl.ANY),
        scratch_shapes=[
            *[pltpu.VMEM((bm * SUBLANES, LANES), F32) for _ in range(6)],
            pltpu.VMEM((D, Fd), BF16),
            pltpu.VMEM((D, Fd), BF16),
            pltpu.VMEM((Fd, D), BF16),
            pltpu.SemaphoreType.DMA((3,)),
            pltpu.SemaphoreType.DMA((3,)),
        ],
    )
    return pl.pallas_call(
        functools.partial(_experts_kernel, bm=bm, n_tok=n_tok),
        grid_spec=grid_spec,
        out_shape=jax.ShapeDtypeStruct((n_tok * TOP_K * SUBLANES, LANES), F32),
        compiler_params=_cparams("arbitrary"),
        name="experts",
    )(blk_e, n_used, tab, row_end, xn, w_gate, b_gate, w_up, b_up, w_down, b_down)


def _combine_kernel(h_ref, y0_ref, y1_ref, y2_ref, y3_ref, g_ref, o_ref):
    tm = h_ref.shape[0]
    g = g_ref[...]
    for s in range(SUBLANES):
        lanes = slice(LANES * s, LANES * (s + 1))
        acc = h_ref[:, lanes]
        for kk, y_ref in enumerate((y0_ref, y1_ref, y2_ref, y3_ref)):
            acc = acc + g[:, kk:kk + 1] * y_ref[pl.ds(s, tm, stride=SUBLANES), :]
        o_ref[:, lanes] = acc


def _combine(h2, y_tiles, gate_t, tm):
    T = h2.shape[0]
    nt = T // tm
    plane = lambda kk: pl.BlockSpec((tm * SUBLANES, LANES), lambda i, kk=kk: (kk * nt + i, 0))
    return pl.pallas_call(
        _combine_kernel,
        grid=(nt,),
        in_specs=[pl.BlockSpec((tm, D_MODEL), lambda i: (i, 0)),
                  plane(0), plane(1), plane(2), plane(3),
                  pl.BlockSpec((tm, TOP_K), lambda i: (i, 0))],
        out_specs=pl.BlockSpec((tm, D_MODEL), lambda i: (i, 0)),
        out_shape=jax.ShapeDtypeStruct((T, D_MODEL), F32),
        compiler_params=_cparams("arbitrary"),
        name="combine",
    )(h2, y_tiles, y_tiles, y_tiles, y_tiles, gate_t)


def _layer(x, meta_tokens, norm_mix_g, w_in, ssm_a_re, ssm_a_im, ssm_log_dt, ssm_b_re, ssm_b_im,
           ssm_c_re, ssm_c_im, ssm_d, w_glu, b_glu, q_norm_g, k_norm_g, lambda_q1, lambda_k1,
           lambda_q2, lambda_k2, subln_g, w_ssm_out, w_attn_out, w_o, norm_ffn_g, w_router,
           b_router, w_gate, b_gate, w_up, b_up, w_down, b_down,
           *, tm=512, tc=256, tq=512, tt=512, bm=256):
    B, L, D = x.shape
    T = B * L
    G, P, C = SSM_GROUPS, SSM_STATE, SSM_GROUP
    o_g = SSM_WIDTH + 3 * ATTN_WIDTH

    w_uqkv = w_in[:, :o_g].astype(BF16)
    w_gates = w_in[:, o_g:].astype(BF16)
    gmix = norm_mix_g.reshape(1, D)
    scale = HEAD_DIM ** -0.5
    qg = (jnp.tile(q_norm_g, 2 * ATTN_HEADS) * scale).reshape(1, ATTN_WIDTH)
    kg = jnp.tile(k_norm_g, 2 * ATTN_HEADS).reshape(1, ATTN_WIDTH)
    gmat = jnp.asarray(np.kron(np.eye(2 * ATTN_HEADS), np.ones((HEAD_DIM, HEAD_DIM))) / HEAD_DIM, BF16)

    x2 = x.reshape(T, D)
    u_x, qt_x, k_x, vt_x = _in_proj(x2, gmix, w_uqkv, qg, kg, gmat, tm)
    meta_pad = jnp.zeros((META_ROWS, D), F32).at[:N_META].set(meta_tokens.astype(F32))
    u_m, _, k_m, vt_m = _in_proj(meta_pad, gmix, w_uqkv, qg, kg, gmat, META_ROWS)

    pw_re, pw_im, bbr_t, bbi_t = _s5_params(
        ssm_a_re.reshape(1, G * P), ssm_a_im.reshape(1, G * P),
        jnp.repeat(ssm_log_dt, P).reshape(1, G * P),
        jnp.transpose(ssm_b_re, (2, 0, 1)).reshape(C, G * P),
        jnp.transpose(ssm_b_im, (2, 0, 1)).reshape(C, G * P))
    bmat, cmat = _s5_matrices(bbr_t, bbi_t, ssm_c_re, ssm_c_im)
    u_meta_tile = jnp.zeros((tc, SSM_WIDTH), F32).at[tc - N_META:].set(u_m[:N_META])
    ys = _s5_scan(u_meta_tile, u_x.reshape(B, L, SSM_WIDTH), bmat, cmat, pw_re, pw_im,
                  ssm_d.reshape(1, SSM_WIDTH), tc)

    lam = (jnp.exp(jnp.sum(lambda_q1 * lambda_k1)) - jnp.exp(jnp.sum(lambda_q2 * lambda_k2))
           + LAMBDA_INIT).reshape(1).astype(F32)
    sub_g_b = jnp.broadcast_to((subln_g * (1.0 - LAMBDA_INIT)).reshape(HEAD_W, 1), (HEAD_W, tq))
    ya = _diff_attn(lam, qt_x, k_x.reshape(B, L, ATTN_WIDTH), vt_x, k_m, vt_m, sub_g_b, tq)

    wrh, wrl = _split_bf16(jnp.transpose(w_router))
    h2, xn, logits_t = _merge(
        x2, ys.reshape(T, SSM_WIDTH), ya.reshape(T, ATTN_WIDTH), gmix, w_gates,
        w_glu.astype(BF16), b_glu.reshape(1, SSM_WIDTH), w_ssm_out.astype(BF16),
        w_attn_out.astype(BF16), w_o.astype(BF16), norm_ffn_g.reshape(1, D),
        wrh, wrl, b_router.reshape(N_EXPERTS, 1), tm)

    n_blocks = T * TOP_K // bm + N_EXPERTS
    gate, dest, cnt = _route(logits_t, tt, bm)
    cnt_i = cnt[:, 0].astype(I32)
    nblk = (cnt_i + bm - 1) // bm
    blk_end = jnp.cumsum(nblk)
    pstart_i = (blk_end - nblk) * bm
    blk_e = jnp.minimum(jnp.sum((blk_end[None, :] <= jnp.arange(n_blocks, dtype=I32)[:, None]).astype(I32),
                                axis=1), N_EXPERTS - 1)
    n_used = blk_end[-1:].astype(I32)
    tab = _invert(dest.reshape(T * TOP_K), cnt_i, pstart_i, T, (n_blocks + 1) * bm, bm)
    y_assign = _experts(blk_e, n_used, tab, pstart_i + cnt_i, xn, w_gate, b_gate.reshape(N_EXPERTS, 1, -1),
                        w_up, b_up.reshape(N_EXPERTS, 1, -1), w_down, b_down.reshape(N_EXPERTS, 1, -1), bm)
    out = _combine(h2, y_assign, jnp.transpose(gate), tm)
    return out.reshape(B, L, D)


def kernel(x, meta_tokens, norm_mix_g, w_in, ssm_a_re, ssm_a_im, ssm_log_dt, ssm_b_re, ssm_b_im,
           ssm_c_re, ssm_c_im, ssm_d, w_glu, b_glu, q_norm_g, k_norm_g, lambda_q1, lambda_k1,
           lambda_q2, lambda_k2, subln_g, w_ssm_out, w_attn_out, w_o, norm_ffn_g, w_router,
           b_router, w_gate, b_gate, w_up, b_up, w_down, b_down):
    p = [a[0] for a in (norm_mix_g, w_in, ssm_a_re, ssm_a_im, ssm_log_dt, ssm_b_re, ssm_b_im,
                        ssm_c_re, ssm_c_im, ssm_d, w_glu, b_glu, q_norm_g, k_norm_g, lambda_q1,
                        lambda_k1, lambda_q2, lambda_k2, subln_g, w_ssm_out, w_attn_out, w_o,
                        norm_ffn_g, w_router, b_router, w_gate, b_gate, w_up, b_up, w_down, b_down)]
    return _layer(x, meta_tokens, *p)
```

```python
import functools
import math

import numpy as np
import jax
import jax.numpy as jnp
from jax import lax
from jax.experimental import pallas as pl
from jax.experimental.pallas import tpu as pltpu

F32 = jnp.float32
BF16 = jnp.bfloat16
I32 = jnp.int32

D_MODEL = 1024
N_META = 16
SSM_WIDTH = 512
SSM_GROUP = 16
SSM_GROUPS = 32
SSM_STATE = 64
A_RE_MAX = -1e-4
ATTN_HEADS = 4
HEAD_DIM = 64
ATTN_WIDTH = 512
NEG_INF = -1e30
N_EXPERTS = 32
TOP_K = 4
SWIGLU_LIMIT = 7.0
SWIGLU_ALPHA = 1.702
EPS = 1e-6
LAMBDA_INIT = 0.8 - 0.6 * math.exp(-0.3 * 0)

LANES = 128
SUBLANES = 8
META_ROWS = 128
VMEM_LIMIT = 56 * 1024 * 1024
DMA_UNROLL = 8

N_STATE = SSM_GROUPS * SSM_STATE
STATE_TILES = N_STATE // LANES
HEAD_W = 2 * HEAD_DIM


def _cparams(*sem):
    return pltpu.CompilerParams(dimension_semantics=sem, vmem_limit_bytes=VMEM_LIMIT)


def _in_proj_kernel(x_ref, g_ref, w_ref, qg_ref, kg_ref, gm_ref, u_ref, qt_ref, k_ref, vt_ref):
    x = x_ref[...]
    ms = jnp.mean(x * x, axis=-1, keepdims=True)
    n = (x * lax.rsqrt(ms + EPS) * g_ref[...]).astype(BF16)
    z = jnp.dot(n, w_ref[...], preferred_element_type=F32)
    u_ref[...] = z[:, :SSM_WIDTH]
    gm = gm_ref[...]

    def head_norm(t, g):
        ms_g = jnp.dot((t * t).astype(BF16), gm, preferred_element_type=F32)
        return t * lax.rsqrt(ms_g + EPS) * g

    q = z[:, SSM_WIDTH:SSM_WIDTH + ATTN_WIDTH]
    k = z[:, SSM_WIDTH + ATTN_WIDTH:SSM_WIDTH + 2 * ATTN_WIDTH]
    qt_ref[...] = jnp.transpose(head_norm(q, qg_ref[...])).astype(BF16)
    k_ref[...] = head_norm(k, kg_ref[...]).astype(BF16)
    vt_ref[...] = jnp.transpose(z[:, SSM_WIDTH + 2 * ATTN_WIDTH:]).astype(BF16)


def _in_proj(x2, g_mix, w_uqkv, qg, kg, gmat, tm):
    n_rows = x2.shape[0]
    wdt = w_uqkv.shape[1]
    row = lambda i: (i, 0)
    col = lambda i: (0, i)
    fixed = lambda i: (0, 0)
    return pl.pallas_call(
        _in_proj_kernel,
        grid=(n_rows // tm,),
        in_specs=[
            pl.BlockSpec((tm, D_MODEL), row),
            pl.BlockSpec((1, D_MODEL), fixed),
            pl.BlockSpec((D_MODEL, wdt), fixed),
            pl.BlockSpec((1, ATTN_WIDTH), fixed),
            pl.BlockSpec((1, ATTN_WIDTH), fixed),
            pl.BlockSpec((ATTN_WIDTH, ATTN_WIDTH), fixed),
        ],
        out_specs=[
            pl.BlockSpec((tm, SSM_WIDTH), row),
            pl.BlockSpec((ATTN_WIDTH, tm), col),
            pl.BlockSpec((tm, ATTN_WIDTH), row),
            pl.BlockSpec((ATTN_WIDTH, tm), col),
        ],
        out_shape=[
            jax.ShapeDtypeStruct((n_rows, SSM_WIDTH), F32),
            jax.ShapeDtypeStruct((ATTN_WIDTH, n_rows), BF16),
            jax.ShapeDtypeStruct((n_rows, ATTN_WIDTH), BF16),
            jax.ShapeDtypeStruct((ATTN_WIDTH, n_rows), BF16),
        ],
        compiler_params=_cparams("arbitrary"),
        name="in_proj",
    )(x2, g_mix, w_uqkv, qg, kg, gmat)


def _s5_params_kernel(are_ref, aim_ref, ldt_ref, bre_ref, bim_ref,
                      pwr_ref, pwi_ref, bbr_ref, bbi_ref):
    lr = jnp.minimum(are_ref[...], A_RE_MAX)
    li = aim_ref[...]
    dt = jnp.exp(ldt_ref[...])
    mag = jnp.exp(lr * dt)
    ar = mag * jnp.cos(li * dt)
    ai = mag * jnp.sin(li * dt)
    den = lr * lr + li * li
    fr = ((ar - 1.0) * lr + ai * li) / den
    fi = (ai * lr - (ar - 1.0) * li) / den
    br = bre_ref[...]
    bi = bim_ref[...]
    bbr_ref[...] = fr * br - fi * bi
    bbi_ref[...] = fr * bi + fi * br
    pr, pi = ar, ai
    pwr_ref[0:1, :] = pr
    pwi_ref[0:1, :] = pi
    for t in range(1, SUBLANES):
        pr, pi = pr * ar - pi * ai, pr * ai + pi * ar
        pwr_ref[t:t + 1, :] = pr
        pwi_ref[t:t + 1, :] = pi


def _s5_params(a_re, a_im, log_dt_b, b_re_t, b_im_t):
    return pl.pallas_call(
        _s5_params_kernel,
        out_shape=[
            jax.ShapeDtypeStruct((SUBLANES, N_STATE), F32),
            jax.ShapeDtypeStruct((SUBLANES, N_STATE), F32),
            jax.ShapeDtypeStruct((SSM_GROUP, N_STATE), F32),
            jax.ShapeDtypeStruct((SSM_GROUP, N_STATE), F32),
        ],
        name="s5_params",
    )(a_re, a_im, log_dt_b, b_re_t, b_im_t)


def _s5_matrices(bbr_t, bbi_t, c_re, c_im):
    G, P, C = SSM_GROUPS, SSM_STATE, SSM_GROUP
    bb = jnp.stack([bbr_t, bbi_t]).reshape(2, C, G, P)
    j = np.arange(STATE_TILES)
    g_of = (2 * j[:, None] + np.arange(2)[None, :])
    v = bb[:, :, g_of, :]
    v = jnp.transpose(v, (2, 1, 0, 3, 4))
    gl = np.arange(8)
    mask_b = (8 * (j[:, None, None] // 4) + gl[None, :, None] == g_of[:, None, :])
    bmat = (jnp.asarray(mask_b, F32)[:, :, None, None, :, None] * v[:, None])
    bmat = bmat.reshape(STATE_TILES, 8 * C, 2 * 2 * P).astype(BF16)

    cc = jnp.stack([c_re, -c_im])
    w = cc[:, g_of, :, :]
    w = jnp.transpose(w, (1, 0, 2, 4, 3))
    gl16 = np.arange(16)
    mask_c = (16 * (j[:, None, None] // 8) + gl16[None, None, :] == g_of[:, :, None])
    cmat = (w[:, :, :, :, None, :] * jnp.asarray(mask_c, F32)[:, None, :, None, :, None])
    cmat = cmat.reshape(STATE_TILES, 2 * 2 * P, 16 * C).astype(BF16)
    return bmat, cmat


def _s5_scan_kernel(um_ref, ux_ref, bmat_ref, cmat_ref, pwr_ref, pwi_ref, d_ref,
                    y_ref, s_ref, cr_ref, ci_ref, *, tc, tiles_per_loop):
    i = pl.program_id(1)

    @pl.when(i == 0)
    def _():
        cr_ref[...] = jnp.zeros_like(cr_ref)
        ci_ref[...] = jnp.zeros_like(ci_ref)

    u = jnp.where(i == 0, um_ref[...], ux_ref[0])
    ub = u.astype(BF16)

    for j in range(STATE_TILES):
        c0 = 2 * LANES * j
        k0 = LANES * (j // 4)
        s_ref[:, c0:c0 + 2 * LANES] = jnp.dot(ub[:, k0:k0 + LANES], bmat_ref[j],
                                              preferred_element_type=F32)

    row = lax.broadcasted_iota(I32, (SUBLANES, LANES), 0)
    for j0 in range(0, STATE_TILES, tiles_per_loop):
        tiles = list(range(j0, j0 + tiles_per_loop))
        consts = []
        for j in tiles:
            lanes = slice(LANES * j, LANES * (j + 1))
            pr = pwr_ref[:, lanes]
            pi = pwi_ref[:, lanes]
            coef = []
            for d in (1, 2, 4):
                m = row >= d
                coef.append((jnp.where(m, jnp.broadcast_to(pr[d - 1:d, :], (SUBLANES, LANES)), 0.0),
                             jnp.where(m, jnp.broadcast_to(pi[d - 1:d, :], (SUBLANES, LANES)), 0.0)))
            consts.append((pr, pi, coef))

        def group(r, carry, tiles=tiles, consts=consts):
            o = pl.multiple_of(r * SUBLANES, SUBLANES)
            out = []
            for j, (pr, pi, coef), (cre, cim) in zip(tiles, consts, carry):
                c0 = 2 * LANES * j
                xr = s_ref[pl.ds(o, SUBLANES), c0:c0 + LANES]
                xi = s_ref[pl.ds(o, SUBLANES), c0 + LANES:c0 + 2 * LANES]
                for d, (kr, ki) in zip((1, 2, 4), coef):
                    sr = pltpu.roll(xr, d, 0)
                    si = pltpu.roll(xi, d, 0)
                    xr, xi = xr + (kr * sr - ki * si), xi + (kr * si + ki * sr)
                xr = xr + (pr * cre - pi * cim)
                xi = xi + (pr * cim + pi * cre)
                s_ref[pl.ds(o, SUBLANES), c0:c0 + LANES] = xr
                s_ref[pl.ds(o, SUBLANES), c0 + LANES:c0 + 2 * LANES] = xi
                out.append((jnp.broadcast_to(xr[SUBLANES - 1:SUBLANES, :], (SUBLANES, LANES)),
                            jnp.broadcast_to(xi[SUBLANES - 1:SUBLANES, :], (SUBLANES, LANES))))
            return tuple(out)

        init = tuple((cr_ref[:, LANES * j:LANES * (j + 1)], ci_ref[:, LANES * j:LANES * (j + 1)])
                     for j in tiles)
        fin = lax.fori_loop(0, tc // SUBLANES, group, init)
        for j, (cre, cim) in zip(tiles, fin):
            cr_ref[:, LANES * j:LANES * (j + 1)] = cre
            ci_ref[:, LANES * j:LANES * (j + 1)] = cim

    half = STATE_TILES // 2
    for n in range(2):
        acc = jnp.zeros((tc, 2 * LANES), F32)
        for j in range(half * n, half * (n + 1)):
            c0 = 2 * LANES * j
            acc += jnp.dot(s_ref[:, c0:c0 + 2 * LANES].astype(BF16), cmat_ref[j],
                           preferred_element_type=F32)
        cols = slice(2 * LANES * n, 2 * LANES * (n + 1))
        y_ref[0, :, cols] = acc + d_ref[:, cols] * u[:, cols]


def _s5_scan(u_meta_tile, u_x, bmat, cmat, pw_re, pw_im, d_row, tc, tiles_per_loop=4):
    B, L, _ = u_x.shape
    nt = L // tc
    xmap = lambda b, i: (b, jnp.maximum(i - 1, 0), 0)
    fixed2 = lambda b, i: (0, 0)
    fixed3 = lambda b, i: (0, 0, 0)
    return pl.pallas_call(
        functools.partial(_s5_scan_kernel, tc=tc, tiles_per_loop=tiles_per_loop),
        grid=(B, nt + 1),
        in_specs=[
            pl.BlockSpec((tc, SSM_WIDTH), fixed2),
            pl.BlockSpec((1, tc, SSM_WIDTH), xmap),
            pl.BlockSpec(bmat.shape, fixed3),
            pl.BlockSpec(cmat.shape, fixed3),
            pl.BlockSpec((SUBLANES, N_STATE), fixed2),
            pl.BlockSpec((SUBLANES, N_STATE), fixed2),
            pl.BlockSpec((1, SSM_WIDTH), fixed2),
        ],
        out_specs=pl.BlockSpec((1, tc, SSM_WIDTH), xmap),
        out_shape=jax.ShapeDtypeStruct((B, L, SSM_WIDTH), F32),
        scratch_shapes=[
            pltpu.VMEM((tc, 2 * N_STATE), F32),
            pltpu.VMEM((SUBLANES, N_STATE), F32),
            pltpu.VMEM((SUBLANES, N_STATE), F32),
        ],
        compiler_params=_cparams("arbitrary", "arbitrary"),
        name="s5_scan",
    )(u_meta_tile, u_x, bmat, cmat, pw_re, pw_im, d_row)


def _attn_kernel(lam_ref, qt_ref, k_ref, vt_ref, km_ref, vmt_ref, g_ref, o_ref,
                 acc_ref, st_ref, sa_ref, sb_ref, bxa_ref, bxb_ref, *, tq):
    qi = pl.program_id(2)
    qt = qt_ref[...]
    feat = lax.broadcasted_iota(I32, qt.shape, 0)
    zero = jnp.zeros_like(qt)
    qtm = (jnp.where(feat < HEAD_DIM, qt, zero), jnp.where(feat >= HEAD_DIM, qt, zero))
    acc_ref[...] = jnp.zeros_like(acc_ref)
    for mi in range(2):
        st_ref[2 * mi:2 * mi + 1, :] = jnp.full((1, tq), NEG_INF, F32)
        st_ref[2 * mi + 1:2 * mi + 2, :] = jnp.zeros((1, tq), F32)
    bufs = ((sa_ref, bxa_ref), (sb_ref, bxb_ref))

    def score(kblk, buf, mask):
        s_ref, bx_ref = bufs[buf]
        rows = kblk.shape[0]
        for mi in range(2):
            s = jnp.dot(kblk, qtm[mi], preferred_element_type=F32)
            if mask is not None:
                s = jnp.where(mask, s, NEG_INF)
            s_ref[mi, :rows, :] = s
            bx_ref[mi:mi + 1, :] = jnp.max(s, axis=0, keepdims=True)

    def update(vtblk, buf):
        s_ref, bx_ref = bufs[buf]
        rows = vtblk.shape[1]
        for mi in range(2):
            m_old = st_ref[2 * mi:2 * mi + 1, :]
            m_new = jnp.maximum(m_old, bx_ref[mi:mi + 1, :])
            alpha = jnp.exp2(m_old - m_new)
            p = jnp.exp2(s_ref[mi, :rows, :] - m_new)
            st_ref[2 * mi + 1:2 * mi + 2, :] = (alpha * st_ref[2 * mi + 1:2 * mi + 2, :]
                                                + jnp.sum(p, axis=0, keepdims=True))
            st_ref[2 * mi:2 * mi + 1, :] = m_new
            acc_ref[mi] = alpha * acc_ref[mi] + jnp.dot(vtblk, p.astype(BF16),
                                                        preferred_element_type=F32)

    def k_block(kb):
        return k_ref[0, pl.ds(pl.multiple_of(kb * tq, tq), tq), :]

    def vt_block(kb):
        return vt_ref[:, pl.ds(pl.multiple_of(kb * tq, tq), tq)]

    key_m = lax.broadcasted_iota(I32, (META_ROWS, tq), 0)
    score(km_ref[...], 0, key_m < N_META)
    update(vmt_ref[...], 0)

    def causal():
        return (lax.broadcasted_iota(I32, (tq, tq), 0) <= lax.broadcasted_iota(I32, (tq, tq), 1))

    @pl.when(qi == 0)
    def _():
        score(k_block(0), 0, causal())
        update(vt_block(0), 0)

    @pl.when(qi > 0)
    def _():
        score(k_block(0), 0, None)
        n_pair = (qi - 1) // 2

        def pair(t, c):
            score(k_block(2 * t + 1), 1, None)
            update(vt_block(2 * t), 0)
            score(k_block(2 * t + 2), 0, None)
            update(vt_block(2 * t + 1), 1)
            return c
        lax.fori_loop(0, n_pair, pair, 0)
        done = 2 * n_pair

        @pl.when(qi - done == 1)
        def _():
            score(k_block(qi), 1, causal())
            update(vt_block(done), 0)
            update(vt_block(qi), 1)

        @pl.when(qi - done == 2)
        def _():
            score(k_block(done + 1), 1, None)
            update(vt_block(done), 0)
            score(k_block(qi), 0, causal())
            update(vt_block(done + 1), 1)
            update(vt_block(qi), 0)

    lam = lam_ref[0]
    out = acc_ref[0] / st_ref[1:2, :] - lam * (acc_ref[1] / st_ref[3:4, :])
    ms = jnp.mean(out * out, axis=0, keepdims=True)
    out = out * lax.rsqrt(ms + EPS) * g_ref[...]
    o_ref[0] = jnp.transpose(out).astype(o_ref.dtype)


def _diff_attn(lam, qt, k, vt, k_meta, vt_meta, sub_g_b, tq):
    B, L, _ = k.shape
    nq = L // tq
    return pl.pallas_call(
        functools.partial(_attn_kernel, tq=tq),
        grid=(B, ATTN_HEADS, nq),
        in_specs=[
            pl.BlockSpec(memory_space=pltpu.SMEM),
            pl.BlockSpec((HEAD_W, tq), lambda b, h, i: (h, b * nq + i)),
            pl.BlockSpec((1, L, HEAD_W), lambda b, h, i: (b, 0, h)),
            pl.BlockSpec((HEAD_W, L), lambda b, h, i: (h, b)),
            pl.BlockSpec((META_ROWS, HEAD_W), lambda b, h, i: (0, h)),
            pl.BlockSpec((HEAD_W, META_ROWS), lambda b, h, i: (h, 0)),
            pl.BlockSpec((HEAD_W, tq), lambda b, h, i: (0, 0)),
        ],
        out_specs=pl.BlockSpec((1, tq, HEAD_W), lambda b, h, i: (b, i, h)),
        out_shape=jax.ShapeDtypeStruct((B, L, ATTN_WIDTH), BF16),
        scratch_shapes=[
            pltpu.VMEM((2, HEAD_W, tq), F32),
            pltpu.VMEM((SUBLANES, tq), F32),
            pltpu.VMEM((2, tq, tq), F32),
            pltpu.VMEM((2, tq, tq), F32),
            pltpu.VMEM((SUBLANES, tq), F32),
            pltpu.VMEM((SUBLANES, tq), F32),
        ],
        compiler_params=_cparams("arbitrary", "arbitrary", "arbitrary"),
        name="diff_attn",
    )(lam, qt, k, vt, k_meta, vt_meta, sub_g_b)


def _split_bf16(x):
    hi = x.astype(BF16)
    lo = (x - hi.astype(F32)).astype(BF16)
    return hi, lo


def _merge_kernel(x_ref, ys_ref, ya_ref, gmix_ref, wg_ref, wglu_ref, bglu_ref, wso_ref, wao_ref,
                  wo_ref, gffn_ref, wrh_ref, wrl_ref, br_ref, h_ref, xn_ref, lg_ref):
    x = x_ref[...]
    ms = jnp.mean(x * x, axis=-1, keepdims=True)
    n = (x * lax.rsqrt(ms + EPS) * gmix_ref[...]).astype(BF16)
    gates = jax.nn.sigmoid(jnp.dot(n, wg_ref[...], preferred_element_type=F32))

    ys = jax.nn.gelu(ys_ref[...])
    ys = ys * jax.nn.sigmoid(jnp.dot(ys.astype(BF16), wglu_ref[...], preferred_element_type=F32)
                             + bglu_ref[...])
    ys = jnp.dot(ys.astype(BF16), wso_ref[...], preferred_element_type=F32)
    ya = jnp.dot(ya_ref[...], wao_ref[...], preferred_element_type=F32)
    mixed = gates[:, :D_MODEL] * ys + gates[:, D_MODEL:] * ya
    h = x + jnp.dot(mixed.astype(BF16), wo_ref[...], preferred_element_type=F32)
    h_ref[...] = h

    ms2 = jnp.mean(h * h, axis=-1, keepdims=True)
    xn = h * lax.rsqrt(ms2 + EPS) * gffn_ref[...]
    tm = xn.shape[0]
    for s in range(SUBLANES):
        xn_ref[pl.ds(s, tm, stride=SUBLANES), :] = xn[:, LANES * s:LANES * (s + 1)]
    xh, xl = _split_bf16(xn)
    nt = (((1,), (1,)), ((), ()))
    lg = (lax.dot_general(wrh_ref[...], xh, nt, preferred_element_type=F32)
          + lax.dot_general(wrh_ref[...], xl, nt, preferred_element_type=F32)
          + lax.dot_general(wrl_ref[...], xh, nt, preferred_element_type=F32))
    lg_ref[...] = lg + br_ref[...]


def _merge(x2, ys, ya, gmix, wg, wglu, bglu, wso, wao, wo, gffn, wrh, wrl, br, tm):
    T = x2.shape[0]
    row = lambda i: (i, 0)
    fixed = lambda i: (0, 0)
    full = lambda a: pl.BlockSpec(a.shape, fixed)
    return pl.pallas_call(
        _merge_kernel,
        grid=(T // tm,),
        in_specs=[
            pl.BlockSpec((tm, D_MODEL), row),
            pl.BlockSpec((tm, SSM_WIDTH), row),
            pl.BlockSpec((tm, ATTN_WIDTH), row),
            full(gmix), full(wg), full(wglu), full(bglu), full(wso), full(wao), full(wo),
            full(gffn), full(wrh), full(wrl), full(br),
        ],
        out_specs=[
            pl.BlockSpec((tm, D_MODEL), row),
            pl.BlockSpec((tm * SUBLANES, LANES), row),
            pl.BlockSpec((N_EXPERTS, tm), lambda i: (0, i)),
        ],
        out_shape=[
            jax.ShapeDtypeStruct((T, D_MODEL), F32),
            jax.ShapeDtypeStruct((T * SUBLANES, LANES), F32),
            jax.ShapeDtypeStruct((N_EXPERTS, T), F32),
        ],
        compiler_params=_cparams("arbitrary"),
        name="merge",
    )(x2, ys, ya, gmix, wg, wglu, bglu, wso, wao, wo, gffn, wrh, wrl, br)


def _route_kernel(lg_ref, tri_ref, ltri_ref, gate_ref, dest_ref, cnt_ref, base_ref, pstart_ref, *, bm):
    ps = pl.program_id(0)
    i = pl.program_id(1)

    @pl.when(jnp.logical_and(ps == 0, i == 0))
    def _():
        base_ref[...] = jnp.zeros_like(base_ref)

    @pl.when(jnp.logical_and(ps == 1, i == 0))
    def _():
        total = base_ref[...]
        cnt_ref[...] = total
        nblk = jnp.floor((total + (bm - 1)) * (1.0 / bm))
        hi = jnp.floor(nblk * (1.0 / 16))
        lo = nblk - 16.0 * hi
        ex_hi = jnp.dot(ltri_ref[...], hi.astype(BF16), preferred_element_type=F32)
        ex_lo = jnp.dot(ltri_ref[...], lo.astype(BF16), preferred_element_type=F32)
        pstart_ref[...] = float(bm) * (16.0 * ex_hi + ex_lo)
        base_ref[...] = jnp.zeros_like(base_ref)

    work = lg_ref[...]
    row = lax.broadcasted_iota(I32, work.shape, 0).astype(F32)
    vals, onehots = [], []
    for _ in range(TOP_K):
        mx = jnp.max(work, axis=0, keepdims=True)
        ek = jnp.min(jnp.where(work == mx, row, float(N_EXPERTS)), axis=0, keepdims=True)
        oh = row == ek
        work = jnp.where(oh, -jnp.inf, work)
        vals.append(mx)
        onehots.append(oh)
    cnt = jnp.zeros(work.shape, F32)
    for oh in onehots:
        cnt = cnt + oh.astype(F32)

    @pl.when(ps == 1)
    def _():
        ex = [jnp.exp(v - vals[0]) for v in vals]
        den = ex[0] + ex[1] + ex[2] + ex[3]
        pre = (jnp.dot(cnt.astype(BF16), tri_ref[...], preferred_element_type=F32)
               + base_ref[:, 0:1] + pstart_ref[:, 0:1])
        for kk in range(TOP_K):
            gate_ref[kk:kk + 1, :] = ex[kk] / den
            dest_ref[kk:kk + 1, :] = jnp.sum(jnp.where(onehots[kk], pre, 0.0), axis=0,
                                             keepdims=True).astype(I32)

    base_ref[...] = base_ref[...] + jnp.sum(cnt, axis=1, keepdims=True)


def _route(logits_t, tt, bm):
    T = logits_t.shape[1]
    tri = jnp.asarray(np.triu(np.ones((tt, tt), np.float32), k=1), BF16)
    ltri = jnp.asarray(np.tril(np.ones((N_EXPERTS, N_EXPERTS), np.float32), k=-1), BF16)
    out_col = lambda ps, i: (0, i * ps)
    return pl.pallas_call(
        functools.partial(_route_kernel, bm=bm),
        grid=(2, T // tt),
        in_specs=[pl.BlockSpec((N_EXPERTS, tt), lambda ps, i: (0, i)),
                  pl.BlockSpec((tt, tt), lambda ps, i: (0, 0)),
                  pl.BlockSpec((N_EXPERTS, N_EXPERTS), lambda ps, i: (0, 0))],
        out_specs=[
            pl.BlockSpec((TOP_K, tt), out_col),
            pl.BlockSpec((TOP_K, tt), out_col),
            pl.BlockSpec((N_EXPERTS, LANES), lambda ps, i: (0, 0)),
        ],
        out_shape=[
            jax.ShapeDtypeStruct((TOP_K, T), F32),
            jax.ShapeDtypeStruct((TOP_K, T), I32),
            jax.ShapeDtypeStruct((N_EXPERTS, LANES), F32),
        ],
        scratch_shapes=[pltpu.VMEM((N_EXPERTS, LANES), F32), pltpu.VMEM((N_EXPERTS, LANES), F32)],
        compiler_params=_cparams("arbitrary", "arbitrary"),
        name="route",
    )(logits_t, tri, ltri)


def _invert_kernel(dest_ref, cnt_ref, pstart_ref, tab_ref, *, T, bm):
    A = T * TOP_K

    def pad_expert(e, c):
        first = pstart_ref[e] + cnt_ref[e]
        last = pstart_ref[e] + (cnt_ref[e] + (bm - 1)) // bm * bm

        def fill(r, c2):
            tab_ref[r] = A + r
            return c2
        lax.fori_loop(first, last, fill, 0)
        return last
    used = lax.fori_loop(0, N_EXPERTS, pad_expert, 0)

    def fill_tail(r, c):
        tab_ref[r] = A + r
        return c
    lax.fori_loop(used, tab_ref.shape[0], fill_tail, 0)

    def chunk(g, c):
        for q in range(DMA_UNROLL):
            a = g * DMA_UNROLL + q
            tab_ref[dest_ref[a]] = a
        return c
    lax.fori_loop(0, A // DMA_UNROLL, chunk, 0)


def _invert(dest_flat, cnt_i, pstart_i, T, n_rows, bm):
    smem = pl.BlockSpec(memory_space=pltpu.SMEM)
    return pl.pallas_call(
        functools.partial(_invert_kernel, T=T, bm=bm),
        in_specs=[smem, smem, smem],
        out_specs=smem,
        out_shape=jax.ShapeDtypeStruct((n_rows,), I32),
        name="invert",
    )(dest_flat, cnt_i, pstart_i)


def _experts_kernel(blke_ref, nused_ref, tab_ref, rend_ref, xn_hbm,
                    wg_ref, bg_ref, wu_ref, bu_ref, wd_ref, bd_ref, ya_hbm,
                    xb0, xb1, xb2, yb0, yb1, yb2, wgb, wub, wdb, gsem, ssem, *, bm, n_tok):
    i = pl.program_id(0)
    nu = nused_ref[0]
    xb = (xb0, xb1, xb2)
    yb = (yb0, yb1, yb2)

    def tile(ref, row):
        return ref.at[pl.ds(pl.multiple_of(row * SUBLANES, SUBLANES), SUBLANES), :]

    def row_in(v, r, s):
        t = v & (n_tok - 1) if n_tok & (n_tok - 1) == 0 else lax.rem(v, n_tok)
        return pltpu.make_async_copy(tile(xn_hbm, t), tile(xb[s], r), gsem.at[s])

    def row_out(v, r, s):
        return pltpu.make_async_copy(tile(yb[s], r), tile(ya_hbm, v), ssem.at[s])

    def valid_rows(blk):
        return jnp.minimum(bm, rend_ref[blke_ref[blk]] - blk * bm)

    def issue_in(blk, s):
        base = blk * bm
        for r in range(bm):
            row_in(tab_ref[base + r], r, s).start(priority=r % 2)

    def issue_out(blk, s):
        base = blk * bm
        nv = valid_rows(blk)
        for r in range(bm):
            @pl.when(r < nv)
            def _(r=r):
                row_out(tab_ref[base + r], r, s).start(priority=r % 2)

    def wait_in(s):
        pltpu.make_async_copy(xn_hbm.at[pl.ds(0, bm * SUBLANES), :], xb[s], gsem.at[s]).wait()

    def wait_out(blk, s):
        n8 = pl.multiple_of(valid_rows(blk) * SUBLANES, SUBLANES)
        pltpu.make_async_copy(yb[s].at[pl.ds(0, n8), :], ya_hbm.at[pl.ds(0, n8), :], ssem.at[s]).wait()

    @pl.when(i == 0)
    def _():
        issue_in(0, 0)
        issue_in(1, 1)

    def step(s, first):
        nxt, prv = (s + 1) % 3, (s + 2) % 3
        changed = jnp.logical_or(i == 0, blke_ref[i] != blke_ref[jnp.maximum(i - 1, 0)])

        @pl.when(changed)
        def _():
            wgb[...] = wg_ref[0].astype(BF16)
            wub[...] = wu_ref[0].astype(BF16)
            wdb[...] = wd_ref[0].astype(BF16)

        wait_in(s)
        if not first:
            @pl.when(i >= 3)
            def _():
                wait_out(i - 3, s)

        issue_in(i + 2, prv)
        if not first:
            issue_out(i - 1, prv)
        hb = jnp.concatenate(
            [xb[s][pl.ds(c, bm, stride=SUBLANES), :] for c in range(SUBLANES)], axis=1).astype(BF16)
        g = jnp.minimum(jnp.dot(hb, wgb[...], preferred_element_type=F32) + bg_ref[0], SWIGLU_LIMIT)
        u = jnp.clip(jnp.dot(hb, wub[...], preferred_element_type=F32) + bu_ref[0],
                     -SWIGLU_LIMIT, SWIGLU_LIMIT)
        a = (u + 1.0) * (g * jax.nn.sigmoid(SWIGLU_ALPHA * g))
        y = jnp.dot(a.astype(BF16), wdb[...], preferred_element_type=F32) + bd_ref[0]
        for c in range(SUBLANES):
            yb[s][pl.ds(c, bm, stride=SUBLANES), :] = y[:, LANES * c:LANES * (c + 1)]

        @pl.when(i == nu - 1)
        def _():
            issue_out(i, s)
            wait_in(nxt)
            wait_in(prv)
            wait_out(i, s)
            if not first:
                wait_out(i - 1, prv)

                @pl.when(i >= 2)
                def _():
                    wait_out(i - 2, nxt)

    @pl.when(i == 0)
    def _():
        step(0, True)

    for s in range(3):
        @pl.when(jnp.logical_and(jnp.logical_and(i > 0, i < nu), i % 3 == s))
        def _(s=s):
            step(s, False)


def _experts(blk_e, n_used, tab, row_end, xn, w_gate, b_gate, w_up, b_up, w_down, b_down, bm):
    n_rows = tab.shape[0]
    n_blocks = n_rows // bm - 2
    n_tok = xn.shape[0] // SUBLANES
    E, D, Fd = w_gate.shape
    assert D == SUBLANES * LANES
    wmap = lambda i, be, nu, tb, re: (be[i], 0, 0)
    grid_spec = pltpu.PrefetchScalarGridSpec(
        num_scalar_prefetch=4,
        grid=(n_blocks,),
        in_specs=[
            pl.BlockSpec(memory_space=pl.ANY),
            pl.BlockSpec((1, D, Fd), wmap),
            pl.BlockSpec((1, 1, Fd), wmap),
            pl.BlockSpec((1, D, Fd), wmap),
            pl.BlockSpec((1, 1, Fd), wmap),
            pl.BlockSpec((1, Fd, D), wmap),
            pl.BlockSpec((1, 1, D), wmap),
        ],
        out_specs=pl.BlockSpec(memory_space=pl.ANY),
        scratch_shapes=[
            *[pltpu.VMEM((bm * SUBLANES, LANES), F32) for _ in range(6)],
            pltpu.VMEM((D, Fd), BF16),
            pltpu.VMEM((D, Fd), BF16),
            pltpu.VMEM((Fd, D), BF16),
            pltpu.SemaphoreType.DMA((3,)),
            pltpu.SemaphoreType.DMA((3,)),
        ],
    )
    return pl.pallas_call(
        functools.partial(_experts_kernel, bm=bm, n_tok=n_tok),
        grid_spec=grid_spec,
        out_shape=jax.ShapeDtypeStruct((n_tok * TOP_K * SUBLANES, LANES), F32),
        compiler_params=_cparams("arbitrary"),
        name="experts",
    )(blk_e, n_used, tab, row_end, xn, w_gate, b_gate, w_up, b_up, w_down, b_down)


def _combine_kernel(h_ref, y0_ref, y1_ref, y2_ref, y3_ref, g_ref, o_ref):
    tm = h_ref.shape[0]
    g = g_ref[...]
    for s in range(SUBLANES):
        lanes = slice(LANES * s, LANES * (s + 1))
        acc = h_ref[:, lanes]
        for kk, y_ref in enumerate((y0_ref, y1_ref, y2_ref, y3_ref)):
            acc = acc + g[:, kk:kk + 1] * y_ref[pl.ds(s, tm, stride=SUBLANES), :]
        o_ref[:, lanes] = acc


def _combine(h2, y_tiles, gate_t, tm):
    T = h2.shape[0]
    nt = T // tm
    plane = lambda kk: pl.BlockSpec((tm * SUBLANES, LANES), lambda i, kk=kk: (kk * nt + i, 0))
    return pl.pallas_call(
        _combine_kernel,
        grid=(nt,),
        in_specs=[pl.BlockSpec((tm, D_MODEL), lambda i: (i, 0)),
                  plane(0), plane(1), plane(2), plane(3),
                  pl.BlockSpec((tm, TOP_K), lambda i: (i, 0))],
        out_specs=pl.BlockSpec((tm, D_MODEL), lambda i: (i, 0)),
        out_shape=jax.ShapeDtypeStruct((T, D_MODEL), F32),
        compiler_params=_cparams("arbitrary"),
        name="combine",
    )(h2, y_tiles, y_tiles, y_tiles, y_tiles, gate_t)


def _layer(x, meta_tokens, norm_mix_g, w_in, ssm_a_re, ssm_a_im, ssm_log_dt, ssm_b_re, ssm_b_im,
           ssm_c_re, ssm_c_im, ssm_d, w_glu, b_glu, q_norm_g, k_norm_g, lambda_q1, lambda_k1,
           lambda_q2, lambda_k2, subln_g, w_ssm_out, w_attn_out, w_o, norm_ffn_g, w_router,
           b_router, w_gate, b_gate, w_up, b_up, w_down, b_down,
           *, tm=512, tc=256, tq=512, tt=512, bm=256):
    B, L, D = x.shape
    T = B * L
    G, P, C = SSM_GROUPS, SSM_STATE, SSM_GROUP
    o_g = SSM_WIDTH + 3 * ATTN_WIDTH

    w_uqkv = w_in[:, :o_g].astype(BF16)
    w_gates = w_in[:, o_g:].astype(BF16)
    gmix = norm_mix_g.reshape(1, D)
    scale = HEAD_DIM ** -0.5 * math.log2(math.e)
    qg =(jnp.tile(q_norm_g, 2 * ATTN_HEADS) * scale).reshape(1, ATTN_WIDTH)
    kg = jnp.tile(k_norm_g, 2 * ATTN_HEADS).reshape(1, ATTN_WIDTH)
    gmat = jnp.asarray(np.kron(np.eye(2 * ATTN_HEADS), np.ones((HEAD_DIM, HEAD_DIM))) / HEAD_DIM, BF16)

    x2 = x.reshape(T, D)
    u_x, qt_x, k_x, vt_x = _in_proj(x2, gmix, w_uqkv, qg, kg, gmat, tm)
    meta_pad = jnp.zeros((META_ROWS, D), F32).at[:N_META].set(meta_tokens.astype(F32))
    u_m, _, k_m, vt_m = _in_proj(meta_pad, gmix, w_uqkv, qg, kg, gmat, META_ROWS)

    pw_re, pw_im, bbr_t, bbi_t = _s5_params(
        ssm_a_re.reshape(1, G * P), ssm_a_im.reshape(1, G * P),
        jnp.repeat(ssm_log_dt, P).reshape(1, G * P),
        jnp.transpose(ssm_b_re, (2, 0, 1)).reshape(C, G * P),
        jnp.transpose(ssm_b_im, (2, 0, 1)).reshape(C, G * P))
    bmat, cmat = _s5_matrices(bbr_t, bbi_t, ssm_c_re, ssm_c_im)
    u_meta_tile = jnp.zeros((tc, SSM_WIDTH), F32).at[tc - N_META:].set(u_m[:N_META])
    ys = _s5_scan(u_meta_tile, u_x.reshape(B, L, SSM_WIDTH), bmat, cmat, pw_re, pw_im,
                  ssm_d.reshape(1, SSM_WIDTH), tc)

    lam = (jnp.exp(jnp.sum(lambda_q1 * lambda_k1)) - jnp.exp(jnp.sum(lambda_q2 * lambda_k2))
           + LAMBDA_INIT).reshape(1).astype(F32)
    sub_g_b = jnp.broadcast_to((subln_g * (1.0 - LAMBDA_INIT)).reshape(HEAD_W, 1), (HEAD_W, tq))
    ya = _diff_attn(lam, qt_x, k_x.reshape(B, L, ATTN_WIDTH), vt_x, k_m, vt_m, sub_g_b, tq)

    wrh, wrl = _split_bf16(jnp.transpose(w_router))
    h2, xn, logits_t = _merge(
        x2, ys.reshape(T, SSM_WIDTH), ya.reshape(T, ATTN_WIDTH), gmix, w_gates,
        w_glu.astype(BF16), b_glu.reshape(1, SSM_WIDTH), w_ssm_out.astype(BF16),
        w_attn_out.astype(BF16), w_o.astype(BF16), norm_ffn_g.reshape(1, D),
        wrh, wrl, b_router.reshape(N_EXPERTS, 1), tm)

    n_blocks = T * TOP_K // bm + N_EXPERTS
    gate, dest, cnt = _route(logits_t, tt, bm)
    cnt_i = cnt[:, 0].astype(I32)
    nblk = (cnt_i + bm - 1) // bm
    blk_end = jnp.cumsum(nblk)
    pstart_i = (blk_end - nblk) * bm
    blk_e = jnp.minimum(jnp.sum((blk_end[None, :] <= jnp.arange(n_blocks, dtype=I32)[:, None]).astype(I32),
                                axis=1), N_EXPERTS - 1)
    n_used = blk_end[-1:].astype(I32)
    tab = _invert(dest.reshape(T * TOP_K), cnt_i, pstart_i, T, (n_blocks + 2) * bm, bm)
    y_assign = _experts(blk_e, n_used, tab, pstart_i + cnt_i, xn, w_gate, b_gate.reshape(N_EXPERTS, 1, -1),
                        w_up, b_up.reshape(N_EXPERTS, 1, -1), w_down, b_down.reshape(N_EXPERTS, 1, -1), bm)
    out = _combine(h2, y_assign, jnp.transpose(gate), tm)
    return out.reshape(B, L, D)


def kernel(x, meta_tokens, norm_mix_g, w_in, ssm_a_re, ssm_a_im, ssm_log_dt, ssm_b_re, ssm_b_im,
           ssm_c_re, ssm_c_im, ssm_d, w_glu, b_glu, q_norm_g, k_norm_g, lambda_q1, lambda_k1,
           lambda_q2, lambda_k2, subln_g, w_ssm_out, w_attn_out, w_o, norm_ffn_g, w_router,
           b_router, w_gate, b_gate, w_up, b_up, w_down, b_down):
    p = [a[0] for a in (norm_mix_g, w_in, ssm_a_re, ssm_a_im, ssm_log_dt, ssm_b_re, ssm_b_im,
                        ssm_c_re, ssm_c_im, ssm_d, w_glu, b_glu, q_norm_g, k_norm_g, lambda_q1,
                        lambda_k1, lambda_q2, lambda_k2, subln_g, w_ssm_out, w_attn_out, w_o,
                        norm_ffn_g, w_router, b_router, w_gate, b_gate, w_up, b_up, w_down, b_down)]
    return _layer(x, meta_tokens, *p)
```

```python
import functools
import math

import numpy as np
import jax
import jax.numpy as jnp
from jax import lax
from jax.experimental import pallas as pl
from jax.experimental.pallas import tpu as pltpu

F32 = jnp.float32
BF16 = jnp.bfloat16
I32 = jnp.int32

D_MODEL = 1024
N_META = 16
SSM_WIDTH = 512
SSM_GROUP = 16
SSM_GROUPS = 32
SSM_STATE = 64
A_RE_MAX = -1e-4
ATTN_HEADS = 4
HEAD_DIM = 64
ATTN_WIDTH = 512
NEG_INF = -1e30
N_EXPERTS = 32
TOP_K = 4
SWIGLU_LIMIT = 7.0
SWIGLU_ALPHA = 1.702
EPS = 1e-6
LAMBDA_INIT = 0.8 - 0.6 * math.exp(-0.3 * 0)

LANES = 128
SUBLANES = 8
META_ROWS = 128
VMEM_LIMIT = 56 * 1024 * 1024
DMA_UNROLL = 8

N_STATE = SSM_GROUPS * SSM_STATE
STATE_TILES = N_STATE // LANES
HEAD_W = 2 * HEAD_DIM


def _cparams(*sem):
    return pltpu.CompilerParams(dimension_semantics=sem, vmem_limit_bytes=VMEM_LIMIT)


def _in_proj_kernel(x_ref, g_ref, w_ref, qg_ref, kg_ref, gm_ref, u_ref, qt_ref, k_ref, vt_ref):
    x = x_ref[...]
    ms = jnp.mean(x * x, axis=-1, keepdims=True)
    n = (x * lax.rsqrt(ms + EPS) * g_ref[...]).astype(BF16)
    z = jnp.dot(n, w_ref[...], preferred_element_type=F32)
    u_ref[...] = z[:, :SSM_WIDTH]
    gm = gm_ref[...]

    def head_norm(t, g):
        ms_g = jnp.dot((t * t).astype(BF16), gm, preferred_element_type=F32)
        return t * lax.rsqrt(ms_g + EPS) * g

    q = z[:, SSM_WIDTH:SSM_WIDTH + ATTN_WIDTH]
    k = z[:, SSM_WIDTH + ATTN_WIDTH:SSM_WIDTH + 2 * ATTN_WIDTH]
    qt_ref[...] = jnp.transpose(head_norm(q, qg_ref[...])).astype(BF16)
    k_ref[...] = head_norm(k, kg_ref[...]).astype(BF16)
    vt_ref[...] = jnp.transpose(z[:, SSM_WIDTH + 2 * ATTN_WIDTH:]).astype(BF16)


def _in_proj(x2, g_mix, w_uqkv, qg, kg, gmat, tm):
    n_rows = x2.shape[0]
    wdt = w_uqkv.shape[1]
    row = lambda i: (i, 0)
    col = lambda i: (0, i)
    fixed = lambda i: (0, 0)
    return pl.pallas_call(
        _in_proj_kernel,
        grid=(n_rows // tm,),
        in_specs=[
            pl.BlockSpec((tm, D_MODEL), row),
            pl.BlockSpec((1, D_MODEL), fixed),
            pl.BlockSpec((D_MODEL, wdt), fixed),
            pl.BlockSpec((1, ATTN_WIDTH), fixed),
            pl.BlockSpec((1, ATTN_WIDTH), fixed),
            pl.BlockSpec((ATTN_WIDTH, ATTN_WIDTH), fixed),
        ],
        out_specs=[
            pl.BlockSpec((tm, SSM_WIDTH), row),
            pl.BlockSpec((ATTN_WIDTH, tm), col),
            pl.BlockSpec((tm, ATTN_WIDTH), row),
            pl.BlockSpec((ATTN_WIDTH, tm), col),
        ],
        out_shape=[
            jax.ShapeDtypeStruct((n_rows, SSM_WIDTH), F32),
            jax.ShapeDtypeStruct((ATTN_WIDTH, n_rows), BF16),
            jax.ShapeDtypeStruct((n_rows, ATTN_WIDTH), BF16),
            jax.ShapeDtypeStruct((ATTN_WIDTH, n_rows), BF16),
        ],
        compiler_params=_cparams("arbitrary"),
        name="in_proj",
    )(x2, g_mix, w_uqkv, qg, kg, gmat)


def _s5_params_kernel(are_ref, aim_ref, ldt_ref, bre_ref, bim_ref,
                      pwr_ref, pwi_ref, pbr_ref, pbi_ref, bbr_ref, bbi_ref, *, nk):
    lr = jnp.minimum(are_ref[...], A_RE_MAX)
    li = aim_ref[...]
    dt = jnp.exp(ldt_ref[...])
    mag = jnp.exp(lr * dt)
    ar = mag * jnp.cos(li * dt)
    ai = mag * jnp.sin(li * dt)
    den = lr * lr + li * li
    fr = ((ar - 1.0) * lr + ai * li) / den
    fi = (ai * lr - (ar - 1.0) * li) / den
    br = bre_ref[...]
    bi = bim_ref[...]
    bbr_ref[...] = fr * br - fi * bi
    bbi_ref[...] = fr * bi + fi * br
    pr, pi = ar, ai
    pwr_ref[0:1, :] = pr
    pwi_ref[0:1, :] = pi
    for t in range(1, nk):
        pr, pi = pr * ar - pi * ai, pr * ai + pi * ar
        pwr_ref[t:t + 1, :] = pr
        pwi_ref[t:t + 1, :] = pi
    pbr_ref[...] = jnp.zeros_like(pbr_ref)
    pbi_ref[...] = jnp.zeros_like(pbi_ref)
    for t in range(3):
        pbr_ref[t:t + 1, :] = pr
        pbi_ref[t:t + 1, :] = pi
        pr, pi = pr * pr - pi * pi, 2.0 * pr * pi


def _s5_params(a_re, a_im, log_dt_b, b_re_t, b_im_t, nk):
    return pl.pallas_call(
        functools.partial(_s5_params_kernel, nk=nk),
        out_shape=[
            jax.ShapeDtypeStruct((nk, N_STATE), F32),
            jax.ShapeDtypeStruct((nk, N_STATE), F32),
            jax.ShapeDtypeStruct((SUBLANES, N_STATE), F32),
            jax.ShapeDtypeStruct((SUBLANES, N_STATE), F32),
            jax.ShapeDtypeStruct((SSM_GROUP, N_STATE), F32),
            jax.ShapeDtypeStruct((SSM_GROUP, N_STATE), F32),
        ],
        name="s5_params",
    )(a_re, a_im, log_dt_b, b_re_t, b_im_t)


def _s5_matrices(bbr_t, bbi_t, c_re, c_im):
    G, P, C = SSM_GROUPS, SSM_STATE, SSM_GROUP
    bb = jnp.stack([bbr_t, bbi_t]).reshape(2, C, G, P)
    j = np.arange(STATE_TILES)
    g_of = (2 * j[:, None] + np.arange(2)[None, :])
    v = bb[:, :, g_of, :]
    v = jnp.transpose(v, (2, 1, 0, 3, 4))
    gl = np.arange(8)
    mask_b = (8 * (j[:, None, None] // 4) + gl[None, :, None] == g_of[:, None, :])
    bmat = (jnp.asarray(mask_b, F32)[:, :, None, None, :, None] * v[:, None])
    bmat = bmat.reshape(STATE_TILES, 8 * C, 2 * 2 * P).astype(BF16)

    cc = jnp.stack([c_re, -c_im])
    w = cc[:, g_of, :, :]
    w = jnp.transpose(w, (1, 0, 2, 4, 3))
    gl16 = np.arange(16)
    mask_c = (16 * (j[:, None, None] // 8) + gl16[None, None, :] == g_of[:, :, None])
    cmat = (w[:, :, :, :, None, :] * jnp.asarray(mask_c, F32)[:, None, :, None, :, None])
    cmat = cmat.reshape(STATE_TILES, 2 * 2 * P, 16 * C).astype(BF16)
    return bmat, cmat


def _s5_scan_kernel(um_ref, ux0_ref, ux1_ref, ux2_ref, ux3_ref, bmat_ref, cmat_ref,
                    pwr_ref, pwi_ref, pbr_ref, pbi_ref, d_ref,
                    y0_ref, y1_ref, y2_ref, y3_ref,
                    up_ref, s_ref, ccr_ref, cci_ref, cr_ref, ci_ref, *, tc, tiles_per_loop):
    i = pl.program_id(1)
    nk = tc // SUBLANES
    ux_refs = (ux0_ref, ux1_ref, ux2_ref, ux3_ref)
    y_refs = (y0_ref, y1_ref, y2_ref, y3_ref)

    @pl.when(i == 0)
    def _():
        cr_ref[...] = jnp.zeros_like(cr_ref)
        ci_ref[...] = jnp.zeros_like(ci_ref)
        up_ref[...] = um_ref[...]

    @pl.when(i > 0)
    def _():
        for c, ux_ref in enumerate(ux_refs):
            for k in range(nk):
                up_ref[SUBLANES * k:SUBLANES * (k + 1), LANES * c:LANES * (c + 1)] = (
                    ux_ref[0, pl.ds(k, SUBLANES, stride=nk), :])

    ub = up_ref[...].astype(BF16)

    for j in range(STATE_TILES):
        c0 = 2 * LANES * j
        k0 = LANES * (j // 4)
        s_ref[:, c0:c0 + 2 * LANES] = jnp.dot(ub[:, k0:k0 + LANES], bmat_ref[j],
                                              preferred_element_type=F32)

    row = lax.broadcasted_iota(I32, (SUBLANES, LANES), 0)

    def cmul_add(xr, xi, ar, ai, br, bi):
        return xr + (ar * br - ai * bi), xi + (ar * bi + ai * br)

    for j0 in range(0, STATE_TILES, tiles_per_loop):
        tiles = list(range(j0, j0 + tiles_per_loop))
        lanes = [slice(LANES * j, LANES * (j + 1)) for j in tiles]
        cols = [(slice(2 * LANES * j, 2 * LANES * j + LANES),
                 slice(2 * LANES * j + LANES, 2 * LANES * (j + 1))) for j in tiles]
        a1 = [(pwr_ref[0:SUBLANES, ln], pwi_ref[0:SUBLANES, ln]) for ln in lanes]

        def local(k, carry, cols=cols, a1=a1):
            o = pl.multiple_of(k * SUBLANES, SUBLANES)
            out = []
            for (cre, cim), (ar, ai), (xr_p, xi_p) in zip(cols, a1, carry):
                xr, xi = cmul_add(s_ref[pl.ds(o, SUBLANES), cre], s_ref[pl.ds(o, SUBLANES), cim],
                                  ar, ai, xr_p, xi_p)
                s_ref[pl.ds(o, SUBLANES), cre] = xr
                s_ref[pl.ds(o, SUBLANES), cim] = xi
                out.append((xr, xi))
            return tuple(out)

        zero = jnp.zeros((SUBLANES, LANES), F32)
        ends = lax.fori_loop(0, nk, local, tuple((zero, zero) for _ in tiles))

        for ln, (er, ei) in zip(lanes, ends):
            xr = jnp.where(row == 0, cr_ref[:, ln], pltpu.roll(er, 1, 0))
            xi = jnp.where(row == 0, ci_ref[:, ln], pltpu.roll(ei, 1, 0))
            for t, d in enumerate((1, 2, 4)):
                m = row >= d
                qr = jnp.where(m, jnp.broadcast_to(pbr_ref[t:t + 1, ln], (SUBLANES, LANES)), 0.0)
                qi = jnp.where(m, jnp.broadcast_to(pbi_ref[t:t + 1, ln], (SUBLANES, LANES)), 0.0)
                xr, xi = cmul_add(xr, xi, qr, qi, pltpu.roll(xr, d, 0), pltpu.roll(xi, d, 0))
            ccr_ref[:, ln] = xr
            cci_ref[:, ln] = xi
            q1r = jnp.broadcast_to(pbr_ref[0:1, ln], (SUBLANES, LANES))
            q1i = jnp.broadcast_to(pbi_ref[0:1, ln], (SUBLANES, LANES))
            fr, fi = cmul_add(er, ei, q1r, q1i, xr, xi)
            cr_ref[:, ln] = jnp.broadcast_to(fr[SUBLANES - 1:SUBLANES, :], (SUBLANES, LANES))
            ci_ref[:, ln] = jnp.broadcast_to(fi[SUBLANES - 1:SUBLANES, :], (SUBLANES, LANES))

    half = STATE_TILES // 2
    for n in range(2):
        acc = jnp.zeros((tc, 2 * LANES), F32)
        for j in range(half * n, half * (n + 1)):
            c0 = 2 * LANES * j
            ln = slice(LANES * j, LANES * (j + 1))
            cr_t = jnp.tile(ccr_ref[:, ln], (nk, 1))
            ci_t = jnp.tile(cci_ref[:, ln], (nk, 1))
            xr, xi = cmul_add(s_ref[:, c0:c0 + LANES], s_ref[:, c0 + LANES:c0 + 2 * LANES],
                              pwr_ref[:, ln], pwi_ref[:, ln], cr_t, ci_t)
            acc += jnp.dot(jnp.concatenate([xr, xi], axis=1).astype(BF16), cmat_ref[j],
                           preferred_element_type=F32)
        cols2 = slice(2 * LANES * n, 2 * LANES * (n + 1))
        y = acc + d_ref[:, cols2] * up_ref[:, cols2]
        for h in range(2):
            y_ref = y_refs[2 * n + h]
            for k in range(nk):
                y_ref[0, pl.ds(k, SUBLANES, stride=nk), :] = (
                    y[SUBLANES * k:SUBLANES * (k + 1), LANES * h:LANES * (h + 1)])


def _s5_scan(u_meta_perm, u_x, bmat, cmat, pwb_re, pwb_im, pb_re, pb_im, d_row, tc, tiles_per_loop=8):
    B, L, _ = u_x.shape
    nt = L // tc
    n_col = SSM_WIDTH // LANES
    fixed2 = lambda b, i: (0, 0)
    fixed3 = lambda b, i: (0, 0, 0)
    col_blk = lambda c: pl.BlockSpec((1, tc, LANES), lambda b, i, c=c: (b, jnp.maximum(i - 1, 0), c))
    out_blk = pl.BlockSpec((1, tc, LANES), lambda b, i: (b, jnp.maximum(i - 1, 0), 0))
    return pl.pallas_call(
        functools.partial(_s5_scan_kernel, tc=tc, tiles_per_loop=tiles_per_loop),
        grid=(B, nt + 1),
        in_specs=[
            pl.BlockSpec((tc, SSM_WIDTH), fixed2),
            *[col_blk(c) for c in range(n_col)],
            pl.BlockSpec(bmat.shape, fixed3),
            pl.BlockSpec(cmat.shape, fixed3),
            pl.BlockSpec((tc, N_STATE), fixed2),
            pl.BlockSpec((tc, N_STATE), fixed2),
            pl.BlockSpec((SUBLANES, N_STATE), fixed2),
            pl.BlockSpec((SUBLANES, N_STATE), fixed2),
            pl.BlockSpec((1, SSM_WIDTH), fixed2),
        ],
        out_specs=[out_blk] * n_col,
        out_shape=[jax.ShapeDtypeStruct((B, L, LANES), F32)] * n_col,
        scratch_shapes=[
            pltpu.VMEM((tc, SSM_WIDTH), F32),
            pltpu.VMEM((tc, 2 * N_STATE), F32),
            pltpu.VMEM((SUBLANES, N_STATE), F32),
            pltpu.VMEM((SUBLANES, N_STATE), F32),
            pltpu.VMEM((SUBLANES, N_STATE), F32),
            pltpu.VMEM((SUBLANES, N_STATE), F32),
        ],
        compiler_params=_cparams("arbitrary", "arbitrary"),
        name="s5_scan",
    )(u_meta_perm, u_x, u_x, u_x, u_x, bmat, cmat, pwb_re, pwb_im, pb_re, pb_im, d_row)


def _attn_kernel(lam_ref, qt_ref, k_ref, vt_ref, km_ref, vmt_ref, g_ref, o_ref,
                 acc_ref, st_ref, sa_ref, sb_ref, bxa_ref, bxb_ref, *, tq):
    qi = pl.program_id(2)
    qt = qt_ref[...]
    feat = lax.broadcasted_iota(I32, qt.shape, 0)
    zero = jnp.zeros_like(qt)
    qtm = (jnp.where(feat < HEAD_DIM, qt, zero), jnp.where(feat >= HEAD_DIM, qt, zero))
    acc_ref[...] = jnp.zeros_like(acc_ref)
    for mi in range(2):
        st_ref[2 * mi:2 * mi + 1, :] = jnp.full((1, tq), NEG_INF, F32)
        st_ref[2 * mi + 1:2 * mi + 2, :] = jnp.zeros((1, tq), F32)
    bufs = ((sa_ref, bxa_ref), (sb_ref, bxb_ref))

    def score(kblk, buf, mask):
        s_ref, bx_ref = bufs[buf]
        rows = kblk.shape[0]
        for mi in range(2):
            s = jnp.dot(kblk, qtm[mi], preferred_element_type=F32)
            if mask is not None:
                s = jnp.where(mask, s, NEG_INF)
            s_ref[mi, :rows, :] = s
            bx_ref[mi:mi + 1, :] = jnp.max(s, axis=0, keepdims=True)

    def update(vtblk, buf):
        s_ref, bx_ref = bufs[buf]
        rows = vtblk.shape[1]
        for mi in range(2):
            m_old = st_ref[2 * mi:2 * mi + 1, :]
            m_new = jnp.maximum(m_old, bx_ref[mi:mi + 1, :])
            alpha = jnp.exp2(m_old - m_new)
            p = jnp.exp2(s_ref[mi, :rows, :] - m_new)
            st_ref[2 * mi + 1:2 * mi + 2, :] = (alpha * st_ref[2 * mi + 1:2 * mi + 2, :]
                                                + jnp.sum(p, axis=0, keepdims=True))
            st_ref[2 * mi:2 * mi + 1, :] = m_new
            acc_ref[mi] = alpha * acc_ref[mi] + jnp.dot(vtblk, p.astype(BF16),
                                                        preferred_element_type=F32)

    def k_block(kb):
        return k_ref[0, pl.ds(pl.multiple_of(kb * tq, tq), tq), :]

    def vt_block(kb):
        return vt_ref[:, pl.ds(pl.multiple_of(kb * tq, tq), tq)]

    key_m = lax.broadcasted_iota(I32, (META_ROWS, tq), 0)
    score(km_ref[...], 0, key_m < N_META)
    update(vmt_ref[...], 0)

    def causal():
        return (lax.broadcasted_iota(I32, (tq, tq), 0) <= lax.broadcasted_iota(I32, (tq, tq), 1))

    @pl.when(qi == 0)
    def _():
        score(k_block(0), 0, causal())
        update(vt_block(0), 0)

    @pl.when(qi > 0)
    def _():
        score(k_block(0), 0, None)
        n_pair = (qi - 1) // 2

        def pair(t, c):
            score(k_block(2 * t + 1), 1, None)
            update(vt_block(2 * t), 0)
            score(k_block(2 * t + 2), 0, None)
            update(vt_block(2 * t + 1), 1)
            return c
        lax.fori_loop(0, n_pair, pair, 0)
        done = 2 * n_pair

        @pl.when(qi - done == 1)
        def _():
            score(k_block(qi), 1, causal())
            update(vt_block(done), 0)
            update(vt_block(qi), 1)

        @pl.when(qi - done == 2)
        def _():
            score(k_block(done + 1), 1, None)
            update(vt_block(done), 0)
            score(k_block(qi), 0, causal())
            update(vt_block(done + 1), 1)
            update(vt_block(qi), 0)

    lam = lam_ref[0]
    out = acc_ref[0] / st_ref[1:2, :] - lam * (acc_ref[1] / st_ref[3:4, :])
    ms = jnp.mean(out * out, axis=0, keepdims=True)
    out = out * lax.rsqrt(ms + EPS) * g_ref[...]
    o_ref[0] = jnp.transpose(out).astype(o_ref.dtype)


def _diff_attn(lam, qt, k, vt, k_meta, vt_meta, sub_g_b, tq):
    B, L, _ = k.shape
    nq = L // tq
    return pl.pallas_call(
        functools.partial(_attn_kernel, tq=tq),
        grid=(B, ATTN_HEADS, nq),
        in_specs=[
            pl.BlockSpec(memory_space=pltpu.SMEM),
            pl.BlockSpec((HEAD_W, tq), lambda b, h, i: (h, b * nq + i)),
            pl.BlockSpec((1, L, HEAD_W), lambda b, h, i: (b, 0, h)),
            pl.BlockSpec((HEAD_W, L), lambda b, h, i: (h, b)),
            pl.BlockSpec((META_ROWS, HEAD_W), lambda b, h, i: (0, h)),
            pl.BlockSpec((HEAD_W, META_ROWS), lambda b, h, i: (h, 0)),
            pl.BlockSpec((HEAD_W, tq), lambda b, h, i: (0, 0)),
        ],
        out_specs=pl.BlockSpec((1, tq, HEAD_W), lambda b, h, i: (b, i, h)),
        out_shape=jax.ShapeDtypeStruct((B, L, ATTN_WIDTH), BF16),
        scratch_shapes=[
            pltpu.VMEM((2, HEAD_W, tq), F32),
            pltpu.VMEM((SUBLANES, tq), F32),
            pltpu.VMEM((2, tq, tq), F32),
            pltpu.VMEM((2, tq, tq), F32),
            pltpu.VMEM((SUBLANES, tq), F32),
            pltpu.VMEM((SUBLANES, tq), F32),
        ],
        compiler_params=_cparams("arbitrary", "arbitrary", "arbitrary"),
        name="diff_attn",
    )(lam, qt, k, vt, k_meta, vt_meta, sub_g_b)


def _split_bf16(x):
    hi = x.astype(BF16)
    lo = (x - hi.astype(F32)).astype(BF16)
    return hi, lo


def _merge_kernel(x_ref, ys0_ref, ys1_ref, ys2_ref, ys3_ref, ya_ref, gmix_ref, wg_ref, wglu_ref,
                  bglu_ref, wso_ref, wao_ref, wo_ref, gffn_ref, wrh_ref, wrl_ref, br_ref,
                  h_ref, xn_ref, lg_ref):
    x = x_ref[...]
    ms = jnp.mean(x * x, axis=-1, keepdims=True)
    n = (x * lax.rsqrt(ms + EPS) * gmix_ref[...]).astype(BF16)
    gates = jax.nn.sigmoid(jnp.dot(n, wg_ref[...], preferred_element_type=F32))

    ys = jax.nn.gelu(jnp.concatenate([ys0_ref[...], ys1_ref[...], ys2_ref[...], ys3_ref[...]], axis=1))
    ys = ys * jax.nn.sigmoid(jnp.dot(ys.astype(BF16), wglu_ref[...], preferred_element_type=F32)
                             + bglu_ref[...])
    ys = jnp.dot(ys.astype(BF16), wso_ref[...], preferred_element_type=F32)
    ya = jnp.dot(ya_ref[...], wao_ref[...], preferred_element_type=F32)
    mixed = gates[:, :D_MODEL] * ys + gates[:, D_MODEL:] * ya
    h = x + jnp.dot(mixed.astype(BF16), wo_ref[...], preferred_element_type=F32)
    h_ref[...] = h

    ms2 = jnp.mean(h * h, axis=-1, keepdims=True)
    xn = h * lax.rsqrt(ms2 + EPS) * gffn_ref[...]
    tm = xn.shape[0]
    for s in range(SUBLANES):
        xn_ref[pl.ds(s, tm, stride=SUBLANES), :] = xn[:, LANES * s:LANES * (s + 1)]
    xh, xl = _split_bf16(xn)
    nt = (((1,), (1,)), ((), ()))
    lg = (lax.dot_general(wrh_ref[...], xh, nt, preferred_element_type=F32)
          + lax.dot_general(wrh_ref[...], xl, nt, preferred_element_type=F32)
          + lax.dot_general(wrl_ref[...], xh, nt, preferred_element_type=F32))
    lg_ref[...] = lg + br_ref[...]


def _merge(x2, ys, ya, gmix, wg, wglu, bglu, wso, wao, wo, gffn, wrh, wrl, br, tm):
    T = x2.shape[0]
    row = lambda i: (i, 0)
    fixed = lambda i: (0, 0)
    full = lambda a: pl.BlockSpec(a.shape, fixed)
    return pl.pallas_call(
        _merge_kernel,
        grid=(T // tm,),
        in_specs=[
            pl.BlockSpec((tm, D_MODEL), row),
            *[pl.BlockSpec((tm, LANES), row) for _ in ys],
            pl.BlockSpec((tm, ATTN_WIDTH), row),
            full(gmix), full(wg), full(wglu), full(bglu), full(wso), full(wao), full(wo),
            full(gffn), full(wrh), full(wrl), full(br),
        ],
        out_specs=[
            pl.BlockSpec((tm, D_MODEL), row),
            pl.BlockSpec((tm * SUBLANES, LANES), row),
            pl.BlockSpec((N_EXPERTS, tm), lambda i: (0, i)),
        ],
        out_shape=[
            jax.ShapeDtypeStruct((T, D_MODEL), F32),
            jax.ShapeDtypeStruct((T * SUBLANES, LANES), F32),
            jax.ShapeDtypeStruct((N_EXPERTS, T), F32),
        ],
        compiler_params=_cparams("arbitrary"),
        name="merge",
    )(x2, *ys, ya, gmix, wg, wglu, bglu, wso, wao, wo, gffn, wrh, wrl, br)


def _route_kernel(lg_ref, tri_ref, ltri_ref, gate_ref, dest_ref, cnt_ref, base_ref, pstart_ref, *, bm):
    ps = pl.program_id(0)
    i = pl.program_id(1)

    @pl.when(jnp.logical_and(ps == 0, i == 0))
    def _():
        base_ref[...] = jnp.zeros_like(base_ref)

    @pl.when(jnp.logical_and(ps == 1, i == 0))
    def _():
        total = base_ref[...]
        cnt_ref[...] = total
        nblk = jnp.floor((total + (bm - 1)) * (1.0 / bm))
        hi = jnp.floor(nblk * (1.0 / 16))
        lo = nblk - 16.0 * hi
        ex_hi = jnp.dot(ltri_ref[...], hi.astype(BF16), preferred_element_type=F32)
        ex_lo = jnp.dot(ltri_ref[...], lo.astype(BF16), preferred_element_type=F32)
        pstart_ref[...] = float(bm) * (16.0 * ex_hi + ex_lo)
        base_ref[...] = jnp.zeros_like(base_ref)

    work = lg_ref[...]
    row = lax.broadcasted_iota(I32, work.shape, 0).astype(F32)
    vals, onehots = [], []
    for _ in range(TOP_K):
        mx = jnp.max(work, axis=0, keepdims=True)
        ek = jnp.min(jnp.where(work == mx, row, float(N_EXPERTS)), axis=0, keepdims=True)
        oh = row == ek
        work = jnp.where(oh, -jnp.inf, work)
        vals.append(mx)
        onehots.append(oh)
    cnt = jnp.zeros(work.shape, F32)
    for oh in onehots:
        cnt = cnt + oh.astype(F32)

    @pl.when(ps == 1)
    def _():
        ex = [jnp.exp(v - vals[0]) for v in vals]
        den = ex[0] + ex[1] + ex[2] + ex[3]
        pre = (jnp.dot(cnt.astype(BF16), tri_ref[...], preferred_element_type=F32)
               + base_ref[:, 0:1] + pstart_ref[:, 0:1])
        for kk in range(TOP_K):
            gate_ref[kk:kk + 1, :] = ex[kk] / den
            dest_ref[kk:kk + 1, :] = jnp.sum(jnp.where(onehots[kk], pre, 0.0), axis=0,
                                             keepdims=True).astype(I32)

    base_ref[...] = base_ref[...] + jnp.sum(cnt, axis=1, keepdims=True)


def _route(logits_t, tt, bm):
    T = logits_t.shape[1]
    tri = jnp.asarray(np.triu(np.ones((tt, tt), np.float32), k=1), BF16)
    ltri = jnp.asarray(np.tril(np.ones((N_EXPERTS, N_EXPERTS), np.float32), k=-1), BF16)
    out_col = lambda ps, i: (0, i * ps)
    return pl.pallas_call(
        functools.partial(_route_kernel, bm=bm),
        grid=(2, T // tt),
        in_specs=[pl.BlockSpec((N_EXPERTS, tt), lambda ps, i: (0, i)),
                  pl.BlockSpec((tt, tt), lambda ps, i: (0, 0)),
                  pl.BlockSpec((N_EXPERTS, N_EXPERTS), lambda ps, i: (0, 0))],
        out_specs=[
            pl.BlockSpec((TOP_K, tt), out_col),
            pl.BlockSpec((TOP_K, tt), out_col),
            pl.BlockSpec((N_EXPERTS, LANES), lambda ps, i: (0, 0)),
        ],
        out_shape=[
            jax.ShapeDtypeStruct((TOP_K, T), F32),
            jax.ShapeDtypeStruct((TOP_K, T), I32),
            jax.ShapeDtypeStruct((N_EXPERTS, LANES), F32),
        ],
        scratch_shapes=[pltpu.VMEM((N_EXPERTS, LANES), F32), pltpu.VMEM((N_EXPERTS, LANES), F32)],
        compiler_params=_cparams("arbitrary", "arbitrary"),
        name="route",
    )(logits_t, tri, ltri)


def _invert_kernel(dest_ref, cnt_ref, pstart_ref, tab_ref, *, T, bm):
    A = T * TOP_K

    def pad_expert(e, c):
        first = pstart_ref[e] + cnt_ref[e]
        last = pstart_ref[e] + (cnt_ref[e] + (bm - 1)) // bm * bm

        def fill(r, c2):
            tab_ref[r] = A + r
            return c2
        lax.fori_loop(first, last, fill, 0)
        return last
    used = lax.fori_loop(0, N_EXPERTS, pad_expert, 0)

    def fill_tail(r, c):
        tab_ref[r] = A + r
        return c
    lax.fori_loop(used, tab_ref.shape[0], fill_tail, 0)

    def chunk(g, c):
        for q in range(DMA_UNROLL):
            a = g * DMA_UNROLL + q
            tab_ref[dest_ref[a]] = a
        return c
    lax.fori_loop(0, A // DMA_UNROLL, chunk, 0)


def _invert(dest_flat, cnt_i, pstart_i, T, n_rows, bm):
    smem = pl.BlockSpec(memory_space=pltpu.SMEM)
    return pl.pallas_call(
        functools.partial(_invert_kernel, T=T, bm=bm),
        in_specs=[smem, smem, smem],
        out_specs=smem,
        out_shape=jax.ShapeDtypeStruct((n_rows,), I32),
        name="invert",
    )(dest_flat, cnt_i, pstart_i)


def _experts_kernel(blke_ref, nused_ref, tab_ref, rend_ref, xn_hbm,
                    wg_ref, bg_ref, wu_ref, bu_ref, wd_ref, bd_ref, ya_hbm,
                    xb0, xb1, xb2, yb0, yb1, yb2, wgb, wub, wdb, gsem, ssem, *, bm, n_tok):
    i = pl.program_id(0)
    nu = nused_ref[0]
    xb = (xb0, xb1, xb2)
    yb = (yb0, yb1, yb2)

    def tile(ref, row):
        return ref.at[pl.ds(pl.multiple_of(row * SUBLANES, SUBLANES), SUBLANES), :]

    def row_in(v, r, s):
        t = v & (n_tok - 1) if n_tok & (n_tok - 1) == 0 else lax.rem(v, n_tok)
        return pltpu.make_async_copy(tile(xn_hbm, t), tile(xb[s], r), gsem.at[s])

    def row_out(v, r, s):
        return pltpu.make_async_copy(tile(yb[s], r), tile(ya_hbm, v), ssem.at[s])

    def valid_rows(blk):
        return jnp.minimum(bm, rend_ref[blke_ref[blk]] - blk * bm)

    def issue_in(blk, s):
        base = blk * bm
        for r in range(bm):
            row_in(tab_ref[base + r], r, s).start(priority=r % 2)

    def issue_out(blk, s, full=False):
        base = blk * bm
        nv = valid_rows(blk)
        for r in range(bm):
            if full:
                row_out(tab_ref[base + r], r, s).start(priority=r % 2)
            else:
                @pl.when(r < nv)
                def _(r=r):
                    row_out(tab_ref[base + r], r, s).start(priority=r % 2)

    def wait_in(s):
        pltpu.make_async_copy(xn_hbm.at[pl.ds(0, bm * SUBLANES), :], xb[s], gsem.at[s]).wait()

    def wait_out(blk, s):
        n8 = pl.multiple_of(valid_rows(blk) * SUBLANES, SUBLANES)
        pltpu.make_async_copy(yb[s].at[pl.ds(0, n8), :], ya_hbm.at[pl.ds(0, n8), :], ssem.at[s]).wait()

    @pl.when(i == 0)
    def _():
        issue_in(0, 0)
        issue_in(1, 1)

    def step(s, first, prev_full=False):
        nxt, prv = (s + 1) % 3, (s + 2) % 3
        changed = jnp.logical_or(i == 0, blke_ref[i] != blke_ref[jnp.maximum(i - 1, 0)])

        @pl.when(changed)
        def _():
            wgb[...] = wg_ref[0].astype(BF16)
            wub[...] = wu_ref[0].astype(BF16)
            wdb[...] = wd_ref[0].astype(BF16)

        wait_in(s)
        if not first:
            @pl.when(i >= 3)
            def _():
                wait_out(i - 3, s)

        issue_in(i + 2, prv)
        if not first:
            issue_out(i - 1, prv, prev_full)
        hb = jnp.concatenate(
            [xb[s][pl.ds(c, bm, stride=SUBLANES), :] for c in range(SUBLANES)], axis=1).astype(BF16)
        g = jnp.minimum(jnp.dot(hb, wgb[...], preferred_element_type=F32) + bg_ref[0], SWIGLU_LIMIT)
        u = jnp.clip(jnp.dot(hb, wub[...], preferred_element_type=F32) + bu_ref[0],
                     -SWIGLU_LIMIT, SWIGLU_LIMIT)
        a = (u + 1.0) * (g * jax.nn.sigmoid(SWIGLU_ALPHA * g))
        y = jnp.dot(a.astype(BF16), wdb[...], preferred_element_type=F32) + bd_ref[0]
        for c in range(SUBLANES):
            yb[s][pl.ds(c, bm, stride=SUBLANES), :] = y[:, LANES * c:LANES * (c + 1)]

        @pl.when(i == nu - 1)
        def _():
            issue_out(i, s)
            wait_in(nxt)
            wait_in(prv)
            wait_out(i, s)
            if not first:
                wait_out(i - 1, prv)

                @pl.when(i >= 2)
                def _():
                    wait_out(i - 2, nxt)

    @pl.when(i == 0)
    def _():
        step(0, True)

    prev_is_full = valid_rows(jnp.maximum(i - 1, 0)) == bm
    for s in range(3):
        for full in (True, False):
            @pl.when(jnp.logical_and(jnp.logical_and(i > 0, i < nu),
                                     jnp.logical_and(i % 3 == s, prev_is_full == full)))
            def _(s=s, full=full):
                step(s, False, full)


def _experts(blk_e, n_used, tab, row_end, xn, w_gate, b_gate, w_up, b_up, w_down, b_down, bm):
    n_rows = tab.shape[0]
    n_blocks = n_rows // bm - 2
    n_tok = xn.shape[0] // SUBLANES
    E, D, Fd = w_gate.shape
    assert D == SUBLANES * LANES
    wmap = lambda i, be, nu, tb, re: (be[i], 0, 0)
    grid_spec = pltpu.PrefetchScalarGridSpec(
        num_scalar_prefetch=4,
        grid=(n_blocks,),
        in_specs=[
            pl.BlockSpec(memory_space=pl.ANY),
            pl.BlockSpec((1, D, Fd), wmap),
            pl.BlockSpec((1, 1, Fd), wmap),
            pl.BlockSpec((1, D, Fd), wmap),
            pl.BlockSpec((1, 1, Fd), wmap),
            pl.BlockSpec((1, Fd, D), wmap),
            pl.BlockSpec((1, 1, D), wmap),
        ],
        out_specs=pl.BlockSpec(memory_space=pl.ANY),
        scratch_shapes=[
            *[pltpu.VMEM((bm * SUBLANES, LANES), F32) for _ in range(6)],
            pltpu.VMEM((D, Fd), BF16),
            pltpu.VMEM((D, Fd), BF16),
            pltpu.VMEM((Fd, D), BF16),
            pltpu.SemaphoreType.DMA((3,)),
            pltpu.SemaphoreType.DMA((3,)),
        ],
    )
    return pl.pallas_call(
        functools.partial(_experts_kernel, bm=bm, n_tok=n_tok),
        grid_spec=grid_spec,
        out_shape=jax.ShapeDtypeStruct((n_tok * TOP_K * SUBLANES, LANES), F32),
        compiler_params=_cparams("arbitrary"),
        name="experts",
    )(blk_e, n_used, tab, row_end, xn, w_gate, b_gate, w_up, b_up, w_down, b_down)


def _combine_kernel(h_ref, y0_ref, y1_ref, y2_ref, y3_ref, g_ref, o_ref):
    tm = h_ref.shape[0]
    g = g_ref[...]
    for s in range(SUBLANES):
        lanes = slice(LANES * s, LANES * (s + 1))
        acc = h_ref[:, lanes]
        for kk, y_ref in enumerate((y0_ref, y1_ref, y2_ref, y3_ref)):
            acc = acc + g[:, kk:kk + 1] * y_ref[pl.ds(s, tm, stride=SUBLANES), :]
        o_ref[:, lanes] = acc


def _combine(h2, y_tiles, gate_t, tm):
    T = h2.shape[0]
    nt = T // tm
    plane = lambda kk: pl.BlockSpec((tm * SUBLANES, LANES), lambda i, kk=kk: (kk * nt + i, 0))
    return pl.pallas_call(
        _combine_kernel,
        grid=(nt,),
        in_specs=[pl.BlockSpec((tm, D_MODEL), lambda i: (i, 0)),
                  plane(0), plane(1), plane(2), plane(3),
                  pl.BlockSpec((tm, TOP_K), lambda i: (i, 0))],
        out_specs=pl.BlockSpec((tm, D_MODEL), lambda i: (i, 0)),
        out_shape=jax.ShapeDtypeStruct((T, D_MODEL), F32),
        compiler_params=_cparams("arbitrary"),
        name="combine",
    )(h2, y_tiles, y_tiles, y_tiles, y_tiles, gate_t)


def _layer(x, meta_tokens, norm_mix_g, w_in, ssm_a_re, ssm_a_im, ssm_log_dt, ssm_b_re, ssm_b_im,
           ssm_c_re, ssm_c_im, ssm_d, w_glu, b_glu, q_norm_g, k_norm_g, lambda_q1, lambda_k1,
           lambda_q2, lambda_k2, subln_g, w_ssm_out, w_attn_out, w_o, norm_ffn_g, w_router,
           b_router, w_gate, b_gate, w_up, b_up, w_down, b_down,
           *, tm=512, tc=256, tq=512, tt=512, bm=256):
    B, L, D = x.shape
    T = B * L
    G, P, C = SSM_GROUPS, SSM_STATE, SSM_GROUP
    o_g = SSM_WIDTH + 3 * ATTN_WIDTH

    w_uqkv = w_in[:, :o_g].astype(BF16)
    w_gates = w_in[:, o_g:].astype(BF16)
    gmix = norm_mix_g.reshape(1, D)
    scale = HEAD_DIM ** -0.5 * math.log2(math.e)
    qg =(jnp.tile(q_norm_g, 2 * ATTN_HEADS) * scale).reshape(1, ATTN_WIDTH)
    kg = jnp.tile(k_norm_g, 2 * ATTN_HEADS).reshape(1, ATTN_WIDTH)
    gmat = jnp.asarray(np.kron(np.eye(2 * ATTN_HEADS), np.ones((HEAD_DIM, HEAD_DIM))) / HEAD_DIM, BF16)

    x2 = x.reshape(T, D)
    u_x, qt_x, k_x, vt_x = _in_proj(x2, gmix, w_uqkv, qg, kg, gmat, tm)
    meta_pad = jnp.zeros((META_ROWS, D), F32).at[:N_META].set(meta_tokens.astype(F32))
    u_m, _, k_m, vt_m = _in_proj(meta_pad, gmix, w_uqkv, qg, kg, gmat, META_ROWS)

    nk = tc // SUBLANES
    pw_re, pw_im, pb_re, pb_im, bbr_t, bbi_t = _s5_params(
        ssm_a_re.reshape(1, G * P), ssm_a_im.reshape(1, G * P),
        jnp.repeat(ssm_log_dt, P).reshape(1, G * P),
        jnp.transpose(ssm_b_re, (2, 0, 1)).reshape(C, G * P),
        jnp.transpose(ssm_b_im, (2, 0, 1)).reshape(C, G * P), nk)
    bmat, cmat = _s5_matrices(bbr_t, bbi_t, ssm_c_re, ssm_c_im)
    u_meta_tile = jnp.zeros((tc, SSM_WIDTH), F32).at[tc - N_META:].set(u_m[:N_META])
    u_meta_perm = u_meta_tile.reshape(SUBLANES, nk, SSM_WIDTH).transpose(1, 0, 2).reshape(tc, SSM_WIDTH)
    ys = _s5_scan(u_meta_perm, u_x.reshape(B, L, SSM_WIDTH), bmat, cmat,
                  jnp.repeat(pw_re, SUBLANES, axis=0), jnp.repeat(pw_im, SUBLANES, axis=0),
                  pb_re, pb_im, ssm_d.reshape(1, SSM_WIDTH), tc)

    lam = (jnp.exp(jnp.sum(lambda_q1 * lambda_k1)) - jnp.exp(jnp.sum(lambda_q2 * lambda_k2))
           + LAMBDA_INIT).reshape(1).astype(F32)
    sub_g_b = jnp.broadcast_to((subln_g * (1.0 - LAMBDA_INIT)).reshape(HEAD_W, 1), (HEAD_W, tq))
    ya = _diff_attn(lam, qt_x, k_x.reshape(B, L, ATTN_WIDTH), vt_x, k_m, vt_m, sub_g_b, tq)

    wrh, wrl = _split_bf16(jnp.transpose(w_router))
    h2, xn, logits_t = _merge(
        x2, [y.reshape(T, LANES) for y in ys], ya.reshape(T, ATTN_WIDTH), gmix, w_gates,
        w_glu.astype(BF16), b_glu.reshape(1, SSM_WIDTH), w_ssm_out.astype(BF16),
        w_attn_out.astype(BF16), w_o.astype(BF16), norm_ffn_g.reshape(1, D),
        wrh, wrl, b_router.reshape(N_EXPERTS, 1), tm)

    n_blocks = T * TOP_K // bm + N_EXPERTS
    gate, dest, cnt = _route(logits_t, tt, bm)
    cnt_i = cnt[:, 0].astype(I32)
    nblk = (cnt_i + bm - 1) // bm
    blk_end = jnp.cumsum(nblk)
    pstart_i = (blk_end - nblk) * bm
    blk_e = jnp.minimum(jnp.sum((blk_end[None, :] <= jnp.arange(n_blocks, dtype=I32)[:, None]).astype(I32),
                                axis=1), N_EXPERTS - 1)
    n_used = blk_end[-1:].astype(I32)
    tab = _invert(dest.reshape(T * TOP_K), cnt_i, pstart_i, T, (n_blocks + 2) * bm, bm)
    y_assign = _experts(blk_e, n_used, tab, pstart_i + cnt_i, xn, w_gate, b_gate.reshape(N_EXPERTS, 1, -1),
                        w_up, b_up.reshape(N_EXPERTS, 1, -1), w_down, b_down.reshape(N_EXPERTS, 1, -1), bm)
    out = _combine(h2, y_assign, jnp.transpose(gate), tm)
    return out.reshape(B, L, D)


def kernel(x, meta_tokens, norm_mix_g, w_in, ssm_a_re, ssm_a_im, ssm_log_dt, ssm_b_re, ssm_b_im,
           ssm_c_re, ssm_c_im, ssm_d, w_glu, b_glu, q_norm_g, k_norm_g, lambda_q1, lambda_k1,
           lambda_q2, lambda_k2, subln_g, w_ssm_out, w_attn_out, w_o, norm_ffn_g, w_router,
           b_router, w_gate, b_gate, w_up, b_up, w_down, b_down):
    p = [a[0] for a in (norm_mix_g, w_in, ssm_a_re, ssm_a_im, ssm_log_dt, ssm_b_re, ssm_b_im,
                        ssm_c_re, ssm_c_im, ssm_d, w_glu, b_glu, q_norm_g, k_norm_g, lambda_q1,
                        lambda_k1, lambda_q2, lambda_k2, subln_g, w_ssm_out, w_attn_out, w_o,
                        norm_ffn_g, w_router, b_router, w_gate, b_gate, w_up, b_up, w_down, b_down)]
    return _layer(x, meta_tokens, *p)
```
